```python
import math
import jax, jax.numpy as jnp
from jax import lax
import numpy as np

D_MODEL = 2048
BATCH = 8
SEQ = 2048
DEPTH = 4
DEC_BATCH = 4
DEC_SEQ = 4096
PAST_LEN = 128

GRID_W = 64
HEAD_DIM = 128
N_Q_HEADS = 8
N_KV_HEADS = 2
GQA_GROUP = N_Q_HEADS // N_KV_HEADS
ATTN_W = N_Q_HEADS * HEAD_DIM
KV_W = N_KV_HEADS * HEAD_DIM
Q_BLOCK = 128
ROPE_THETA = 10000.0
ROPE_FREQS = HEAD_DIM // 4
POOL_WINDOWS = (2, 4, 8, 16)
N_POOL = len(POOL_WINDOWS)
POOL_DG = 128
POOL_W = N_POOL * POOL_DG
NA_HEADS = 4
NA_W = NA_HEADS * HEAD_DIM
NA_ROWS = 8
NA_COLS = 16
NA_BIAS = (2 * NA_ROWS - 1) * (2 * NA_COLS - 1)
N_BRANCH = 3
IN_SECTIONS = (ATTN_W, KV_W, KV_W, POOL_W, NA_W, NA_W, NA_W, N_BRANCH * D_MODEL)
IN_W = sum(IN_SECTIONS)
SPLIT_AT = [sum(IN_SECTIONS[:i + 1]) for i in range(len(IN_SECTIONS) - 1)]
N_EXPERTS = 32
TOP_K = 4
D_FF = D_MODEL
SWIGLU_ALPHA = 1.702
SWIGLU_LIMIT = 7.0
MOE_BLOCK = 256
EPS = 1e-6

kernel_name = 'hybrid_gqa_pool_natten_moe_encoder'


def rms_norm(x, g):
    xf = x.astype(jnp.float32)
    y = xf * lax.rsqrt(jnp.mean(xf * xf, axis=-1, keepdims=True) + EPS)
    return (y * g.astype(jnp.float32)).astype(x.dtype)


def axial_rope_tables(S):
    t = jnp.arange(S)
    pos = jnp.stack([t // GRID_W, t % GRID_W], axis=-1).astype(jnp.float32)
    inv = ROPE_THETA ** (-jnp.arange(ROPE_FREQS, dtype=jnp.float32) / ROPE_FREQS)
    ang = pos[..., None] * inv
    return jnp.cos(ang), jnp.sin(ang)


def apply_axial_rope(x, cos, sin):
    B, S, H, _ = x.shape
    xs = x.astype(jnp.float32).reshape(B, S, H, 2, 2, ROPE_FREQS)
    x1, x2 = xs[..., 0, :], xs[..., 1, :]
    c, s = cos[None, :, None], sin[None, :, None]
    out = jnp.stack([x1 * c - x2 * s, x2 * c + x1 * s], axis=-2)
    return out.reshape(x.shape).astype(x.dtype)


def gqa_block_sweep(q, k, v):
    B, S = q.shape[:2]
    nb = S // Q_BLOCK
    scale = 1.0 / math.sqrt(HEAD_DIM)
    qb = jnp.moveaxis(q.reshape(B, nb, Q_BLOCK, N_KV_HEADS, GQA_GROUP, HEAD_DIM), 1, 0)

    def one_block(q_blk):
        s = jnp.einsum('bqhgd,bkhd->bhgqk', q_blk, k).astype(jnp.float32) * scale
        p = jax.nn.softmax(s, axis=-1).astype(v.dtype)
        return jnp.einsum('bhgqk,bkhd->bqhgd', p, v)

    o = lax.map(one_block, qb)
    return jnp.moveaxis(o, 0, 1).reshape(B, S, ATTN_W)


def multiscale_pool(u, w_pool, pool_scale):
    B, S, _ = u.shape
    ug = u.reshape(B, S, N_POOL, POOL_DG).astype(jnp.float32)
    cs = jnp.concatenate([jnp.zeros((B, 1, N_POOL, POOL_DG), jnp.float32), jnp.cumsum(ug, axis=1)], axis=1)
    t = jnp.arange(S)[:, None]
    half = jnp.array(POOL_WINDOWS, dtype=jnp.int32)[None, :] // 2
    lo = jnp.clip(t - half, 0, S)
    hi = jnp.clip(t + half, 0, S)
    gidx = jnp.arange(N_POOL)[None, :]
    win_sum = cs[:, hi, gidx] - cs[:, lo, gidx]
    mean = win_sum / (hi - lo).astype(jnp.float32)[None, :, :, None]
    pooled = (mean - ug).astype(u.dtype)
    mixed = jnp.einsum('bsgc,gcd->bsgd', pooled, w_pool)
    return mixed.reshape(B, S, POOL_W) * pool_scale


def na_tables(S):
    R = S // GRID_W
    kh = min(NA_ROWS, R)
    qr = np.arange(R)[:, None, None, None]
    qc = np.arange(GRID_W)[None, :, None, None]
    rs = np.clip(qr - kh // 2, 0, R - kh)
    cs = np.clip(qc - NA_COLS // 2, 0, GRID_W - NA_COLS)
    kr = rs + np.arange(kh)[None, None, :, None]
    kc = cs + np.arange(NA_COLS)[None, None, None, :]
    kr, kc = np.broadcast_arrays(kr, kc)
    key_idx = (kr * GRID_W + kc).reshape(R, GRID_W, kh * NA_COLS)
    bias_idx = ((kr - qr + NA_ROWS - 1) * (2 * NA_COLS - 1) + (kc - qc + NA_COLS - 1)).reshape(R, GRID_W, kh * NA_COLS)
    return jnp.asarray(key_idx.astype(np.int32)), jnp.asarray(bias_idx.astype(np.int32))


def neighbourhood_attention(q, k, v, rpb):
    B, S = q.shape[:2]
    R = S // GRID_W
    scale = 1.0 / math.sqrt(HEAD_DIM)
    key_idx, bias_idx = na_tables(S)
    q_rows = jnp.moveaxis(q.reshape(B, R, GRID_W, NA_HEADS, HEAD_DIM), 1, 0)

    def one_row(args):
        q_row, kidx, bidx = args
        kg = k[:, kidx]
        vg = v[:, kidx]
        s = jnp.einsum('bqhd,bqkhd->bhqk', q_row, kg).astype(jnp.float32) * scale
        s = s + rpb[:, bidx].astype(jnp.float32)[None]
        p = jax.nn.softmax(s, axis=-1).astype(v.dtype)
        return jnp.einsum('bhqk,bqkhd->bqhd', p, vg)

    o = lax.map(one_row, (q_rows, key_idx, bias_idx))
    return jnp.moveaxis(o, 0, 1).reshape(B, S, NA_W)


def token_mixer(h, w_in, q_norm_g, k_norm_g, w_pool, pool_scale, rpb, w_br_attn, w_br_pool, w_br_na, w_out):
    B, S, _ = h.shape
    proj = h @ w_in
    qa, ka, va, u, qn, kn, vn, g = jnp.split(proj, SPLIT_AT, axis=-1)
    cos, sin = axial_rope_tables(S)
    qa = apply_axial_rope(rms_norm(qa.reshape(B, S, N_Q_HEADS, HEAD_DIM), q_norm_g), cos, sin)
    ka = apply_axial_rope(rms_norm(ka.reshape(B, S, N_KV_HEADS, HEAD_DIM), k_norm_g), cos, sin)
    va = va.reshape(B, S, N_KV_HEADS, HEAD_DIM)
    o_attn = gqa_block_sweep(qa.reshape(B, S, N_KV_HEADS, GQA_GROUP, HEAD_DIM), ka, va)
    o_pool = multiscale_pool(u, w_pool, pool_scale)
    shp = (B, S, NA_HEADS, HEAD_DIM)
    o_na = neighbourhood_attention(qn.reshape(shp), kn.reshape(shp), vn.reshape(shp), rpb)
    g_attn, g_pool, g_na = jnp.split(jax.nn.sigmoid(g), N_BRANCH, axis=-1)
    merged = g_attn * (o_attn @ w_br_attn) + g_pool * (o_pool @ w_br_pool) + g_na * (o_na @ w_br_na)
    return merged @ w_out


def moe_ffn(h, w_router, b_router, w_gu, b_gu, w_down, b_down):
    n_tok = h.shape[0]
    logits = (h @ w_router).astype(jnp.float32) + b_router.astype(jnp.float32)
    top_logit, top_e = lax.top_k(logits, TOP_K)
    top_w = jax.nn.softmax(top_logit, axis=-1).astype(h.dtype)
    n_assign = n_tok * TOP_K
    flat_e = top_e.reshape(-1)
    order = jnp.argsort(flat_e)
    sorted_e = flat_e[order]
    counts = jnp.bincount(flat_e, length=N_EXPERTS)
    padded = (counts + MOE_BLOCK - 1) // MOE_BLOCK * MOE_BLOCK
    pad_end = jnp.cumsum(padded)
    pad_start = pad_end - padded
    grp_start = jnp.cumsum(counts) - counts
    dest = pad_start[sorted_e] + jnp.arange(n_assign) - grp_start[sorted_e]
    n_blocks = -(-n_assign // MOE_BLOCK) + N_EXPERTS
    n_slots = n_blocks * MOE_BLOCK
    slot_tok = jnp.zeros((n_slots,), jnp.int32).at[dest].set((order // TOP_K).astype(jnp.int32))
    slot_w = jnp.zeros((n_slots,), h.dtype).at[dest].set(top_w.reshape(-1)[order])
    block_e = jnp.minimum(jnp.searchsorted(pad_end, jnp.arange(n_blocks) * MOE_BLOCK, side='right'), N_EXPERTS - 1)

    def expert_block(args):
        tok, e = args
        xb = h[tok]
        gu = xb @ w_gu[e] + b_gu[e]
        gate = jnp.minimum(gu[:, 0::2], SWIGLU_LIMIT)
        up = jnp.clip(gu[:, 1::2], -SWIGLU_LIMIT, SWIGLU_LIMIT)
        act = (up + 1.0) * gate * jax.nn.sigmoid(SWIGLU_ALPHA * gate)
        return act @ w_down[e] + b_down[e]

    y_slots = lax.map(expert_block, (slot_tok.reshape(n_blocks, MOE_BLOCK), block_e))
    y = jnp.zeros_like(h).at[slot_tok].add(slot_w[:, None] * y_slots.reshape(n_slots, -1))
    return y


def trunk(x, c, p):
    for l in range(DEPTH):
        mod = (jax.nn.silu(c) @ p['w_ada'][l] + p['b_ada'][l])[:, None, :]
        sh1, sc1, g1, sh2, sc2, g2 = jnp.split(mod, 6, axis=-1)
        h = rms_norm(x, p['norm_mix_g'][l]) * (1.0 + sc1) + sh1
        x = x + g1 * token_mixer(h, p['w_in'][l], p['q_norm_g'][l], p['k_norm_g'][l], p['w_pool'][l],
                                 p['pool_scale'][l], p['rpb'][l], p['w_br_attn'][l], p['w_br_pool'][l],
                                 p['w_br_na'][l], p['w_out'][l])
        h = rms_norm(x, p['norm_ffn_g'][l]) * (1.0 + sc2) + sh2
        B, S, D = h.shape
        y = moe_ffn(h.reshape(B * S, D), p['w_router'][l], p['b_router'][l], p['w_gu'][l], p['b_gu'][l],
                    p['w_down'][l], p['b_down'][l])
        x = x + g2 * y.reshape(B, S, D)
    return rms_norm(x, p['final_g'])


def setup_inputs(seed: int = 0) -> dict:
    key = jax.random.key(seed)
    ks = jax.random.split(key, 32)
    f32 = jnp.float32

    def nrm(k, shape, scale):
        return jax.random.normal(k, shape, f32) * scale

    D, L, E = D_MODEL, DEPTH, N_EXPERTS
    return {
        'x_prompt': nrm(ks[0], (BATCH, SEQ, D), 1.0),
        'x_sample': nrm(ks[1], (DEC_BATCH, DEC_SEQ, D), 1.0),
        'c_prompt': nrm(ks[2], (BATCH, D), 1.0),
        'c_sample': nrm(ks[3], (DEC_BATCH, D), 1.0),
        'w_ada': nrm(ks[4], (L, D, 6 * D), 0.3 * D ** -0.5),
        'b_ada': nrm(ks[5], (L, 6 * D), 0.02),
        'norm_mix_g': 1.0 + nrm(ks[6], (L, D), 0.05),
        'norm_ffn_g': 1.0 + nrm(ks[7], (L, D), 0.05),
        'w_in': nrm(ks[8], (L, D, IN_W), D ** -0.5),
        'q_norm_g': 1.0 + nrm(ks[9], (L, HEAD_DIM), 0.05),
        'k_norm_g': 1.0 + nrm(ks[10], (L, HEAD_DIM), 0.05),
        'w_pool': nrm(ks[11], (L, N_POOL, POOL_DG, POOL_DG), POOL_DG ** -0.5),
        'pool_scale': 1.0 + nrm(ks[12], (L, POOL_W), 0.1),
        'rpb': nrm(ks[13], (L, NA_HEADS, NA_BIAS), 0.1),
        'w_br_attn': nrm(ks[14], (L, ATTN_W, D), ATTN_W ** -0.5),
        'w_br_pool': nrm(ks[15], (L, POOL_W, D), POOL_W ** -0.5),
        'w_br_na': nrm(ks[16], (L, NA_W, D), NA_W ** -0.5),
        'w_out': nrm(ks[17], (L, D, D), D ** -0.5),
        'w_router': nrm(ks[18], (L, D, E), D ** -0.5),
        'b_router': nrm(ks[19], (L, E), 0.01),
        'w_gu': nrm(ks[20], (L, E, D, 2 * D_FF), D ** -0.5),
        'b_gu': nrm(ks[21], (L, E, 2 * D_FF), 0.02),
        'w_down': nrm(ks[22], (L, E, D_FF, D), D_FF ** -0.5),
        'b_down': nrm(ks[23], (L, E, D), 0.02),
        'final_g': 1.0 + nrm(ks[24], (D,), 0.05),
    }


def reference(x_prompt, x_sample, c_prompt, c_sample, w_ada, b_ada, norm_mix_g, norm_ffn_g, w_in, q_norm_g,
              k_norm_g, w_pool, pool_scale, rpb, w_br_attn, w_br_pool, w_br_na, w_out, w_router, b_router,
              w_gu, b_gu, w_down, b_down, final_g):
    params = {
        'w_ada': w_ada, 'b_ada': b_ada, 'norm_mix_g': norm_mix_g, 'norm_ffn_g': norm_ffn_g,
        'w_in': w_in, 'q_norm_g': q_norm_g, 'k_norm_g': k_norm_g, 'w_pool': w_pool,
        'pool_scale': pool_scale, 'rpb': rpb, 'w_br_attn': w_br_attn, 'w_br_pool': w_br_pool,
        'w_br_na': w_br_na, 'w_out': w_out, 'w_router': w_router, 'b_router': b_router,
        'w_gu': w_gu, 'b_gu': b_gu, 'w_down': w_down, 'b_down': b_down, 'final_g': final_g,
    }
    y_prompt = trunk(x_prompt, c_prompt, params)
    y_sample = trunk(x_sample, c_sample, params)
    return (y_prompt, y_sample)
```

```python
import functools
import math

import numpy as np
import jax
import jax.numpy as jnp
from jax import lax
from jax.experimental import pallas as pl
from jax.experimental.pallas import tpu as pltpu

HEAD_DIM = 128
N_Q_HEADS = 8
N_KV_HEADS = 2
GQA_GROUP = N_Q_HEADS // N_KV_HEADS
ATTN_W = N_Q_HEADS * HEAD_DIM
KV_W = N_KV_HEADS * HEAD_DIM
GRID_W = 64
ROPE_THETA = 10000.0
ROPE_FREQS = HEAD_DIM // 4
POOL_WINDOWS = (2, 4, 8, 16)
N_POOL = len(POOL_WINDOWS)
POOL_DG = 128
POOL_W = N_POOL * POOL_DG
NA_HEADS = 4
NA_W = NA_HEADS * HEAD_DIM
NA_ROWS = 8
NA_COLS = 16
TOP_K = 4
MOE_BLOCK = 256
SWIGLU_ALPHA = 1.702
SWIGLU_LIMIT = 7.0
EPS = 1e-6

LANES = 128
POOL_HALO = 16
NEG = -1e30
VMEM_LIMIT = 56 * 1024 * 1024

F32 = jnp.float32
BF16 = jnp.bfloat16
_dot = functools.partial(jnp.dot, preferred_element_type=jnp.float32)


def _dot_nt(a, b):
    return lax.dot_general(a, b, (((1,), (1,)), ((), ())), preferred_element_type=jnp.float32)


def _cp(*sem, **kw):
    return pltpu.CompilerParams(dimension_semantics=sem, vmem_limit_bytes=VMEM_LIMIT, **kw)


def _split_bf16(a):
    hi = a.astype(BF16)
    lo = (a - hi.astype(F32)).astype(BF16)
    return hi, lo


def _resident(shape, index_map):
    return pl.BlockSpec(shape, index_map, pipeline_mode=pl.Buffered(1))


def _ada_kernel(c_ref, w_ref, b_ref, o_ref):
    c = c_ref[...]
    a = c * jax.nn.sigmoid(c)
    ah, al = _split_bf16(a)
    wh, wl = _split_bf16(w_ref[...])
    o_ref[...] = _dot(ah, wh) + _dot(al, wh) + _dot(ah, wl) + b_ref[...]


def _ada(c_all, w_ada, b_ada):
    L, D, D6 = w_ada.shape
    Bp = c_all.shape[0]
    tn = math.gcd(1024, D6)
    return pl.pallas_call(
        _ada_kernel,
        grid=(L, D6 // tn),
        in_specs=[pl.BlockSpec((Bp, D), lambda l, j: (0, 0)),
                  pl.BlockSpec((None, D, tn), lambda l, j: (l, 0, j)),
                  pl.BlockSpec((None, 1, tn), lambda l, j: (l, 0, j))],
        out_specs=pl.BlockSpec((None, Bp, tn), lambda l, j: (l, 0, j)),
        out_shape=jax.ShapeDtypeStruct((L, Bp, D6), F32),
        compiler_params=_cp("parallel", "parallel"),
        name="ada",
    )(c_all, w_ada, b_ada.reshape(L, 1, D6))


def _rms_mod(x, g, sc, sh):
    r = lax.rsqrt(jnp.mean(x * x, axis=-1, keepdims=True) + EPS)
    return (x * r * g) * (1.0 + sc) + sh


def _norm_mod_kernel(x_ref, g_ref, sc_ref, sh_ref, o_ref):
    o_ref[...] = _rms_mod(x_ref[...], g_ref[...], sc_ref[...], sh_ref[...]).astype(o_ref.dtype)


def _mod_spec(D, b0, which):
    return pl.BlockSpec((None, None, 1, D), lambda b, s: (b + b0, which, 0, 0))


def _norm_mod(x, g, mod, b0, i_sc, i_sh):
    B, S, D = x.shape
    ts = min(512, S)
    return pl.pallas_call(
        _norm_mod_kernel,
        grid=(B, S // ts),
        in_specs=[pl.BlockSpec((None, ts, D), lambda b, s: (b, s, 0)),
                  pl.BlockSpec((1, D), lambda b, s: (0, 0)),
                  _mod_spec(D, b0, i_sc), _mod_spec(D, b0, i_sh)],
        out_specs=pl.BlockSpec((None, ts, D), lambda b, s: (b, s, 0)),
        out_shape=jax.ShapeDtypeStruct((B, S, D), BF16),
        compiler_params=_cp("parallel", "parallel"),
        name="norm_mod",
    )(x, g.reshape(1, D), mod, mod)


def _final_norm_kernel(x_ref, g_ref, o_ref):
    x = x_ref[...]
    r = lax.rsqrt(jnp.mean(x * x, axis=-1, keepdims=True) + EPS)
    o_ref[...] = x * r * g_ref[...]


def _final_norm(x, g):
    B, S, D = x.shape
    ts = min(512, S)
    return pl.pallas_call(
        _final_norm_kernel,
        grid=(B, S // ts),
        in_specs=[pl.BlockSpec((None, ts, D), lambda b, s: (b, s, 0)),
                  pl.BlockSpec((1, D), lambda b, s: (0, 0))],
        out_specs=pl.BlockSpec((None, ts, D), lambda b, s: (b, s, 0)),
        out_shape=jax.ShapeDtypeStruct((B, S, D), F32),
        compiler_params=_cp("parallel", "parallel"),
        name="final_norm",
    )(x, g.reshape(1, D))


def _rope_tables(S):
    t = jnp.arange(S)
    pos = jnp.stack([t // GRID_W, t % GRID_W], axis=-1).astype(F32)
    inv = ROPE_THETA ** (-jnp.arange(ROPE_FREQS, dtype=F32) / ROPE_FREQS)
    ang = pos[..., None] * inv
    cos, sin = jnp.cos(ang), jnp.sin(ang)
    zero = jnp.zeros_like(sin)
    c = jnp.stack([cos, cos], axis=2).reshape(S, HEAD_DIM)
    sa = jnp.stack([-sin, zero], axis=2).reshape(S, HEAD_DIM)
    sb = jnp.stack([zero, sin], axis=2).reshape(S, HEAD_DIM)
    return c, sa, sb


def _proj_qk_kernel(a_ref, w_ref, gq_ref, gk_ref, c_ref, sa_ref, sb_ref, q_ref, k_ref):
    acc = _dot(a_ref[...], w_ref[...])
    c, sa, sb = c_ref[...], sa_ref[...], sb_ref[...]
    scale = 1.0 / math.sqrt(HEAD_DIM)
    for hh in range(N_Q_HEADS + N_KV_HEADS):
        xh = acc[:, hh * HEAD_DIM:(hh + 1) * HEAD_DIM]
        r = lax.rsqrt(jnp.mean(xh * xh, axis=-1, keepdims=True) + EPS)
        is_q = hh < N_Q_HEADS
        xn = xh * r * (gq_ref[...] if is_q else gk_ref[...])
        y = xn * c + pltpu.roll(xn, HEAD_DIM - ROPE_FREQS, 1) * sa + pltpu.roll(xn, ROPE_FREQS, 1) * sb
        if is_q:
            q_ref[:, hh * HEAD_DIM:(hh + 1) * HEAD_DIM] = (y * scale).astype(q_ref.dtype)
        else:
            kk = hh - N_Q_HEADS
            k_ref[:, kk * HEAD_DIM:(kk + 1) * HEAD_DIM] = y.astype(k_ref.dtype)


def _proj_qk(h, w_qk, gq, gk, tabs):
    B, S, D = h.shape
    tm = min(1024, S)
    W = ATTN_W + KV_W
    tab_spec = pl.BlockSpec((tm, HEAD_DIM), lambda b, s: (s, 0))
    vec_spec = pl.BlockSpec((1, HEAD_DIM), lambda b, s: (0, 0))
    return pl.pallas_call(
        _proj_qk_kernel,
        grid=(B, S // tm),
        in_specs=[pl.BlockSpec((None, tm, D), lambda b, s: (b, s, 0)),
                  _resident((D, W), lambda b, s: (0, 0)),
                  vec_spec, vec_spec, tab_spec, tab_spec, tab_spec],
        out_specs=[pl.BlockSpec((None, tm, ATTN_W), lambda b, s: (b, s, 0)),
                   pl.BlockSpec((None, tm, KV_W), lambda b, s: (b, s, 0))],
        out_shape=[jax.ShapeDtypeStruct((B, S, ATTN_W), BF16),
                   jax.ShapeDtypeStruct((B, S, KV_W), BF16)],
        compiler_params=_cp("parallel", "parallel"),
        name="proj_qk",
    )(h, w_qk, gq.reshape(1, HEAD_DIM), gk.reshape(1, HEAD_DIM), *tabs)


MID_SECTIONS = (KV_W, POOL_W, NA_W, NA_W, NA_W)


def _proj_mid_kernel(a_ref, w_ref, *o_refs):
    acc = _dot(a_ref[...], w_ref[...])
    off = 0
    for o_ref, width in zip(o_refs, MID_SECTIONS):
        o_ref[...] = acc[:, off:off + width].astype(o_ref.dtype)
        off += width


def _proj_mid(h, w_mid):
    B, S, D = h.shape
    tm = min(1024, S)
    W = sum(MID_SECTIONS)
    return pl.pallas_call(
        _proj_mid_kernel,
        grid=(B, S // tm),
        in_specs=[pl.BlockSpec((None, tm, D), lambda b, s: (b, s, 0)),
                  _resident((D, W), lambda b, s: (0, 0))],
        out_specs=[pl.BlockSpec((None, tm, w), lambda b, s: (b, s, 0)) for w in MID_SECTIONS],
        out_shape=[jax.ShapeDtypeStruct((B, S, w), BF16) for w in MID_SECTIONS],
        compiler_params=_cp("parallel", "parallel"),
        name="proj_mid",
    )(h, w_mid)


def _proj_gate_kernel(a_ref, w_ref, o_ref):
    o_ref[...] = jax.nn.sigmoid(_dot(a_ref[...], w_ref[...])).astype(o_ref.dtype)


def _proj_gate(h, w_gate):
    N, D = h.shape
    W = w_gate.shape[1]
    tm = min(1024, N)
    tn = min(2048, D)
    return pl.pallas_call(
        _proj_gate_kernel,
        grid=(W // tn, N // tm),
        in_specs=[pl.BlockSpec((tm, D), lambda j, i: (i, 0)),
                  pl.BlockSpec((D, tn), lambda j, i: (0, j))],
        out_specs=pl.BlockSpec((tm, tn), lambda j, i: (i, j)),
        out_shape=jax.ShapeDtypeStruct((N, W), BF16),
        compiler_params=_cp("parallel", "parallel"),
        name="proj_gate",
    )(h, w_gate)


def _attn_kernel(q_ref, k_ref, v_ref, o_ref, *, tk):
    S = k_ref.shape[0]
    tq = q_ref.shape[0]
    for h in range(GQA_GROUP):
        q = q_ref[:, h * HEAD_DIM:(h + 1) * HEAD_DIM]

        def body(j, carry, q=q):
            m, l, acc = carry
            start = pl.multiple_of(j * tk, tk)
            k = k_ref[pl.ds(start, tk), :]
            v = v_ref[pl.ds(start, tk), :]
            s = _dot_nt(q, k)
            m_new = jnp.maximum(m, jnp.max(s, axis=-1, keepdims=True))
            alpha = jnp.exp(m - m_new)
            p = jnp.exp(s - m_new)
            l = alpha * l + jnp.sum(p, axis=-1, keepdims=True)
            acc = alpha * acc + _dot(p.astype(BF16), v)
            return m_new, l, acc

        init = (jnp.full((tq, 1), NEG, F32), jnp.zeros((tq, 1), F32), jnp.zeros((tq, HEAD_DIM), F32))
        m, l, acc = lax.fori_loop(0, S // tk, body, init)
        o_ref[:, h * HEAD_DIM:(h + 1) * HEAD_DIM] = (acc / l).astype(o_ref.dtype)


def _attention(q, k, v):
    B, S, _ = q.shape
    tq = min(512, S)
    tk = min(512, S)
    gw = GQA_GROUP * HEAD_DIM
    return pl.pallas_call(
        functools.partial(_attn_kernel, tk=tk),
        grid=(B, N_KV_HEADS, S // tq),
        in_specs=[pl.BlockSpec((None, tq, gw), lambda b, g, i: (b, i, g)),
                  pl.BlockSpec((None, S, HEAD_DIM), lambda b, g, i: (b, 0, g)),
                  pl.BlockSpec((None, S, HEAD_DIM), lambda b, g, i: (b, 0, g))],
        out_specs=pl.BlockSpec((None, tq, gw), lambda b, g, i: (b, i, g)),
        out_shape=jax.ShapeDtypeStruct((B, S, ATTN_W), BF16),
        compiler_params=_cp("parallel", "parallel", "parallel"),
        name="gqa",
    )(q, k, v)


def _pool_band():
    T = MOE_BLOCK
    i = np.arange(T)[:, None] + POOL_HALO
    j = np.arange(T + 2 * POOL_HALO)[None, :]
    return np.stack([((j >= i - w // 2) & (j < i + w // 2)) for w in POOL_WINDOWS]).astype(np.float32)


def _pool_kernel(u_ref, band_ref, wp_ref, ps_ref, o_ref, *, T):
    S = u_ref.shape[0]
    nc = S // T
    halo = jnp.zeros((POOL_HALO, POOL_DG), BF16)
    for g in range(N_POOL):
        hw = POOL_WINDOWS[g] // 2
        cols = slice(g * POOL_DG, (g + 1) * POOL_DG)
        for c in range(nc):
            t0 = c * T
            mid = u_ref[t0:t0 + T, cols]
            prev = u_ref[t0 - POOL_HALO:t0, cols] if c > 0 else halo
            nxt = u_ref[t0 + T:t0 + T + POOL_HALO, cols] if c < nc - 1 else halo
            win = _dot(band_ref[g], jnp.concatenate([prev, mid, nxt], axis=0))
            t = t0 + lax.broadcasted_iota(jnp.int32, (T, POOL_DG), 0)
            cnt = (jnp.minimum(t + hw, S) - jnp.maximum(t - hw, 0)).astype(F32)
            pooled = win / cnt - mid.astype(F32)
            mixed = _dot(pooled.astype(BF16), wp_ref[g]) * ps_ref[:, cols]
            o_ref[t0:t0 + T, cols] = mixed.astype(o_ref.dtype)


def _pool(u, band, w_pool, pool_scale):
    B, S, _ = u.shape
    T = band.shape[1]
    return pl.pallas_call(
        functools.partial(_pool_kernel, T=T),
        grid=(B,),
        in_specs=[pl.BlockSpec((None, S, POOL_W), lambda b: (b, 0, 0)),
                  pl.BlockSpec(band.shape, lambda b: (0, 0, 0)),
                  pl.BlockSpec(w_pool.shape, lambda b: (0, 0, 0)),
                  pl.BlockSpec((1, POOL_W), lambda b: (0, 0))],
        out_specs=pl.BlockSpec((None, S, POOL_W), lambda b: (b, 0, 0)),
        out_shape=jax.ShapeDtypeStruct((B, S, POOL_W), BF16),
        compiler_params=_cp("parallel"),
        name="pool",
    )(u, band, w_pool, pool_scale.reshape(1, POOL_W))


def _na_bias_index():
    o = np.arange(NA_ROWS)[:, None, None, None]
    qc = np.arange(GRID_W)[None, :, None, None]
    j = np.arange(NA_ROWS)[None, None, :, None]
    kc = np.arange(GRID_W)[None, None, None, :]
    cs = np.clip(qc - NA_COLS // 2, 0, GRID_W - NA_COLS)
    valid = (kc >= cs) & (kc < cs + NA_COLS)
    idx = (j - o + NA_ROWS - 1) * (2 * NA_COLS - 1) + np.clip(kc - qc + NA_COLS - 1, 0, 2 * NA_COLS - 2)
    idx, valid = np.broadcast_arrays(idx, valid)
    shape = (NA_ROWS, GRID_W, NA_ROWS * GRID_W)
    return idx.reshape(shape).astype(np.int32), valid.reshape(shape)


def _na_kernel(q_ref, k_ref, v_ref, b_ref, o_ref, *, rb, R):
    i = pl.program_id(1)
    scale = 1.0 / math.sqrt(HEAD_DIM)
    win = NA_ROWS * GRID_W
    for rr in range(rb):
        r = i * rb + rr
        rs = jnp.clip(r - NA_ROWS // 2, 0, R - NA_ROWS)
        off = r - rs
        start = pl.multiple_of(rs * GRID_W, GRID_W)
        rows = slice(rr * GRID_W, (rr + 1) * GRID_W)
        for h in range(NA_HEADS):
            cols = slice(h * HEAD_DIM, (h + 1) * HEAD_DIM)
            s = _dot_nt(q_ref[rows, cols], k_ref[pl.ds(start, win), cols]) * scale + b_ref[h, off]
            m = jnp.max(s, axis=-1, keepdims=True)
            p = jnp.exp(s - m)
            l = jnp.sum(p, axis=-1, keepdims=True)
            out = _dot(p.astype(BF16), v_ref[pl.ds(start, win), cols]) / l
            o_ref[rows, cols] = out.astype(o_ref.dtype)


def _na(q, k, v, bias):
    B, S, _ = q.shape
    R = S // GRID_W
    assert R >= NA_ROWS
    rb = 8
    return pl.pallas_call(
        functools.partial(_na_kernel, rb=rb, R=R),
        grid=(B, R // rb),
        in_specs=[pl.BlockSpec((None, rb * GRID_W, NA_W), lambda b, i: (b, i, 0)),
                  pl.BlockSpec((None, S, NA_W), lambda b, i: (b, 0, 0)),
                  pl.BlockSpec((None, S, NA_W), lambda b, i: (b, 0, 0)),
                  pl.BlockSpec(bias.shape, lambda b, i: (0, 0, 0, 0))],
        out_specs=pl.BlockSpec((None, rb * GRID_W, NA_W), lambda b, i: (b, i, 0)),
        out_shape=jax.ShapeDtypeStruct((B, S, NA_W), BF16),
        compiler_params=_cp("parallel", "parallel"),
        name="natten",
    )(q, k, v, bias)


def _merge_kernel(oa_ref, op_ref, on_ref, ga_ref, gp_ref, gn_ref, x_ref, g1_ref,
                  wa_ref, wp_ref, wn_ref, wo_ref, o_ref):
    m = ga_ref[...].astype(F32) * _dot(oa_ref[...], wa_ref[...])
    m = m + gp_ref[...].astype(F32) * _dot(op_ref[...], wp_ref[...])
    m = m + gn_ref[...].astype(F32) * _dot(on_ref[...], wn_ref[...])
    y = _dot(m.astype(BF16), wo_ref[...])
    o_ref[...] = x_ref[...] + g1_ref[...] * y


def _merge(o_attn, o_pool, o_na, gates, x, mod, b0, w_a, w_p, w_n, w_o):
    B, S, D = x.shape
    tm = min(512, S)
    row = lambda w: pl.BlockSpec((None, tm, w), lambda b, s: (b, s, 0))
    gate = lambda j: pl.BlockSpec((None, tm, D), lambda b, s: (b, s, j))
    res = lambda w: _resident(w.shape, lambda b, s: (0, 0))
    return pl.pallas_call(
        _merge_kernel,
        grid=(B, S // tm),
        in_specs=[row(ATTN_W), row(POOL_W), row(NA_W), gate(0), gate(1), gate(2), row(D),
                  _mod_spec(D, b0, 2), res(w_a), res(w_p), res(w_n), res(w_o)],
        out_specs=row(D),
        out_shape=jax.ShapeDtypeStruct((B, S, D), F32),
        compiler_params=_cp("parallel", "parallel"),
        name="merge",
    )(o_attn, o_pool, o_na, gates, gates, gates, x, mod, w_a, w_p, w_n, w_o)


def _ffn_prep_kernel(x_ref, g_ref, sc_ref, sh_ref, wrh_ref, wrl_ref, br_ref,
                     h_ref, te_ref, tw_ref, rk_ref, cnt_ref, carry_ref):
    first = jnp.logical_and(pl.program_id(0) == 0, pl.program_id(1) == 0)

    @pl.when(first)
    def _():
        carry_ref[...] = jnp.zeros_like(carry_ref)

    tm = x_ref.shape[0]
    h = _rms_mod(x_ref[...], g_ref[...], sc_ref[...], sh_ref[...])
    h_ref[...] = h
    hh, hl = _split_bf16(h)
    wrh = wrh_ref[...]
    logits = _dot(hh, wrh) + _dot(hl, wrh) + _dot(hh, wrl_ref[...]) + br_ref[...]
    lane = lax.broadcasted_iota(jnp.int32, (tm, LANES), 1).astype(F32)
    vals, idxs, hots = [], [], []
    cur = logits
    for _ in range(TOP_K):
        m = jnp.max(cur, axis=-1, keepdims=True)
        idx = jnp.min(jnp.where(cur == m, lane, float(LANES)), axis=-1, keepdims=True)
        hot = lane == idx
        vals.append(m)
        idxs.append(idx)
        hots.append(hot)
        cur = jnp.where(hot, -3e38, cur)
    ex = [jnp.exp(v - vals[0]) for v in vals]
    den = ex[0] + ex[1] + ex[2] + ex[3]
    onehot = sum(hh_.astype(F32) for hh_ in hots)
    ri = lax.broadcasted_iota(jnp.int32, (tm, tm), 0)
    ci = lax.broadcasted_iota(jnp.int32, (tm, tm), 1)
    tri = (ri > ci).astype(BF16)
    carry = carry_ref[0:1, :]
    before = _dot(tri, onehot.astype(BF16)) + carry
    te = jnp.zeros((tm, LANES), F32)
    tw = jnp.zeros((tm, LANES), F32)
    rk = jnp.zeros((tm, LANES), F32)
    for k in range(TOP_K):
        sel = lane == float(k)
        rank = jnp.sum(jnp.where(hots[k], before, 0.0), axis=-1, keepdims=True)
        te = jnp.where(sel, idxs[k], te)
        tw = jnp.where(sel, ex[k] / den, tw)
        rk = jnp.where(sel, rank, rk)
    te_ref[...] = te.astype(jnp.int32)
    tw_ref[...] = tw
    rk_ref[...] = rk.astype(jnp.int32)
    total = carry + jnp.sum(onehot, axis=0, keepdims=True)
    carry_ref[...] = jnp.broadcast_to(total, carry_ref.shape)
    cnt_ref[...] = jnp.broadcast_to(total, cnt_ref.shape).astype(jnp.int32)


def _ffn_prep(x, g, mod, b0, wr_hi, wr_lo, br):
    B, S, D = x.shape
    tm = min(512, S)
    row = lambda w: pl.BlockSpec((None, tm, w), lambda b, s: (b, s, 0))
    const = lambda shape: pl.BlockSpec(shape, lambda b, s: (0, 0))
    return pl.pallas_call(
        _ffn_prep_kernel,
        grid=(B, S // tm),
        in_specs=[row(D), const((1, D)), _mod_spec(D, b0, 4), _mod_spec(D, b0, 3),
                  const((D, LANES)), const((D, LANES)), const((1, LANES))],
        out_specs=[row(D), row(LANES), row(LANES), row(LANES), const((8, LANES))],
        out_shape=[jax.ShapeDtypeStruct((B, S, D), F32),
                   jax.ShapeDtypeStruct((B, S, LANES), jnp.int32),
                   jax.ShapeDtypeStruct((B, S, LANES), F32),
                   jax.ShapeDtypeStruct((B, S, LANES), jnp.int32),
                   jax.ShapeDtypeStruct((8, LANES), jnp.int32)],
        scratch_shapes=[pltpu.VMEM((8, LANES), F32)],
        compiler_params=_cp("arbitrary", "arbitrary"),
        name="ffn_prep",
    )(x, g.reshape(1, D), mod, mod, wr_hi, wr_lo, br)


def _dispatch_kernel(nv_ref, dest_ref, h_ref, xs_ref, zeros, sem, zsem):
    tm = h_ref.shape[0]
    nb = nv_ref.shape[0]

    @pl.when(pl.program_id(0) == 0)
    def _():
        zeros[...] = jnp.zeros_like(zeros)

        def tail_copies(b, fn):
            nv = nv_ref[b]
            base = b * MOE_BLOCK
            up = (nv + 7) // 8 * 8
            for r in range(7):
                @pl.when(nv + r < up)
                def _(r=r):
                    fn(pltpu.make_async_copy(zeros.at[pl.ds(0, 1), :], xs_ref.at[pl.ds(base + nv + r, 1), :], zsem))
            units = (MOE_BLOCK - up) // 8
            off = up
            for bit in (32, 16, 8, 4, 2, 1):
                size = bit * 8
                hit = (units & bit) != 0

                @pl.when(hit)
                def _(size=size, off=off):
                    dst = xs_ref.at[pl.ds(pl.multiple_of(base + off, 8), size), :]
                    fn(pltpu.make_async_copy(zeros.at[pl.ds(0, size), :], dst, zsem))
                off = off + jnp.where(hit, size, 0)

        def start_all(b, c):
            tail_copies(b, lambda cp: cp.start())
            return c

        def wait_all(b, c):
            tail_copies(b, lambda cp: cp.wait())
            return c

        lax.fori_loop(0, nb, start_all, 0)
        lax.fori_loop(0, nb, wait_all, 0)

    def row_copy(r, d):
        return pltpu.make_async_copy(h_ref.at[pl.ds(r, 1), :], xs_ref.at[pl.ds(d, 1), :], sem)

    def issue(r, c):
        for k in range(TOP_K):
            row_copy(r, dest_ref[r * TOP_K + k]).start()
        return c

    lax.fori_loop(0, tm, issue, 0)

    def drain(r, c):
        row_copy(0, 0).wait()
        return c

    lax.fori_loop(0, tm * TOP_K, drain, 0)


def _dispatch(h, dest, n_valid):
    N, D = h.shape
    tm = min(512, N)
    n_slots = n_valid.shape[0] * MOE_BLOCK
    return pl.pallas_call(
        _dispatch_kernel,
        grid_spec=pltpu.PrefetchScalarGridSpec(
            num_scalar_prefetch=1, grid=(N // tm,),
            in_specs=[pl.BlockSpec((tm * TOP_K,), lambda i, nv: (i,), memory_space=pltpu.SMEM),
                      pl.BlockSpec((tm, D), lambda i, nv: (i, 0))],
            out_specs=pl.BlockSpec(memory_space=pl.ANY),
            scratch_shapes=[pltpu.VMEM((MOE_BLOCK, D), F32), pltpu.SemaphoreType.DMA(()),
                            pltpu.SemaphoreType.DMA(())]),
        out_shape=jax.ShapeDtypeStruct((n_slots, D), F32),
        compiler_params=_cp("arbitrary", has_side_effects=True),
        name="dispatch",
    )(n_valid, dest, h)


def _gm1_kernel(be_ref, nv_ref, nu_ref, x_ref, wg_ref, wu_ref, bg_ref, bu_ref, o_ref):
    i = pl.program_id(0)

    @pl.when(i < nu_ref[0])
    def _():
        rows = lax.broadcasted_iota(jnp.int32, x_ref.shape, 0)
        x = jnp.where(rows < nv_ref[i], x_ref[...], 0.0).astype(BF16)
        g = _dot(x, wg_ref[...]) + bg_ref[...]
        u = _dot(x, wu_ref[...]) + bu_ref[...]
        gate = jnp.minimum(g, SWIGLU_LIMIT)
        up = jnp.clip(u, -SWIGLU_LIMIT, SWIGLU_LIMIT)
        o_ref[...] = ((up + 1.0) * gate * jax.nn.sigmoid(SWIGLU_ALPHA * gate)).astype(o_ref.dtype)

    @pl.when(i >= nu_ref[0])
    def _():
        o_ref[...] = jnp.zeros_like(o_ref)


def _gm1(xs, block_e, n_valid, n_used, w_g, w_u, b_g, b_u):
    n_slots, D = xs.shape
    E, _, Fd = w_g.shape
    nb = n_slots // MOE_BLOCK
    wspec = pl.BlockSpec((None, D, Fd), lambda i, be, nv, nu: (be[i], 0, 0))
    bspec = pl.BlockSpec((None, 1, Fd), lambda i, be, nv, nu: (be[i], 0, 0))
    return pl.pallas_call(
        _gm1_kernel,
        grid_spec=pltpu.PrefetchScalarGridSpec(
            num_scalar_prefetch=3, grid=(nb,),
            in_specs=[pl.BlockSpec((MOE_BLOCK, D), lambda i, be, nv, nu: (i, 0)), wspec, wspec, bspec, bspec],
            out_specs=pl.BlockSpec((MOE_BLOCK, Fd), lambda i, be, nv, nu: (i, 0))),
        out_shape=jax.ShapeDtypeStruct((n_slots, Fd), BF16),
        compiler_params=_cp("arbitrary"),
        name="expert_up",
    )(block_e, n_valid, n_used, xs, w_g, w_u, b_g, b_u)


def _gm2_kernel(be_ref, nu_ref, a_ref, wd_ref, bd_ref, o_ref):
    i = pl.program_id(0)

    @pl.when(i < nu_ref[0])
    def _():
        o_ref[...] = _dot(a_ref[...], wd_ref[...]) + bd_ref[...]

    @pl.when(i >= nu_ref[0])
    def _():
        o_ref[...] = jnp.zeros_like(o_ref)


def _gm2(act, block_e, n_used, w_d, b_d):
    n_slots, Fd = act.shape
    E, _, D = w_d.shape
    nb = n_slots // MOE_BLOCK
    return pl.pallas_call(
        _gm2_kernel,
        grid_spec=pltpu.PrefetchScalarGridSpec(
            num_scalar_prefetch=2, grid=(nb,),
            in_specs=[pl.BlockSpec((MOE_BLOCK, Fd), lambda i, be, nu: (i, 0)),
                      pl.BlockSpec((None, Fd, D), lambda i, be, nu: (be[i], 0, 0)),
                      pl.BlockSpec((None, 1, D), lambda i, be, nu: (be[i], 0, 0))],
            out_specs=pl.BlockSpec((MOE_BLOCK, D), lambda i, be, nu: (i, 0))),
        out_shape=jax.ShapeDtypeStruct((n_slots, D), F32),
        compiler_params=_cp("arbitrary"),
        name="expert_down",
    )(block_e, n_used, act, w_d, b_d)


def _combine_kernel(dest_ref, x_ref, tw_ref, g2_ref, y_ref, o_ref, buf, sem):
    tm = x_ref.shape[0]

    def row_copy(r, k, d):
        return pltpu.make_async_copy(y_ref.at[pl.ds(d, 1), :], buf.at[k, pl.ds(r, 1), :], sem)

    def issue(r, c):
        for k in range(TOP_K):
            row_copy(r, k, dest_ref[r * TOP_K + k]).start()
        return c

    lax.fori_loop(0, tm, issue, 0)

    def drain(r, c):
        row_copy(0, 0, 0).wait()
        return c

    lax.fori_loop(0, tm * TOP_K, drain, 0)
    tw = tw_ref[...]
    acc = tw[:, 0:1] * buf[0]
    for k in range(1, TOP_K):
        acc = acc + tw[:, k:k + 1] * buf[k]
    o_ref[...] = x_ref[...] + g2_ref[...] * acc


def _combine(x, tw, mod, b0, y_slots, dest):
    B, S, D = x.shape
    tm = min(256, S)
    spb = S // tm
    row = lambda w: pl.BlockSpec((None, tm, w), lambda b, s: (b, s, 0))
    return pl.pallas_call(
        _combine_kernel,
        grid=(B, spb),
        in_specs=[pl.BlockSpec((tm * TOP_K,), lambda b, s: (b * spb + s,), memory_space=pltpu.SMEM),
                  row(D), row(LANES), _mod_spec(D, b0, 5),
                  pl.BlockSpec(memory_space=pl.ANY)],
        out_specs=row(D),
        out_shape=jax.ShapeDtypeStruct((B, S, D), F32),
        scratch_shapes=[pltpu.VMEM((TOP_K, tm, D), F32), pltpu.SemaphoreType.DMA(())],
        compiler_params=_cp("arbitrary", "arbitrary"),
        name="combine",
    )(dest, x, tw, mod, y_slots)


def _slot_tables(te, rk, cnt, n_blocks, E):
    counts = cnt[0, :E]
    padded = (counts + MOE_BLOCK - 1) // MOE_BLOCK * MOE_BLOCK
    pad_end = jnp.cumsum(padded)
    pad_start = pad_end - padded
    dest = (pad_start[te] + rk).reshape(-1).astype(jnp.int32)
    blk0 = jnp.arange(n_blocks, dtype=jnp.int32) * MOE_BLOCK
    block_e = jnp.minimum(jnp.searchsorted(pad_end, blk0, side='right'), E - 1).astype(jnp.int32)
    n_valid = jnp.clip(counts[block_e] - (blk0 - pad_start[block_e]), 0, MOE_BLOCK).astype(jnp.int32)
    n_used = (pad_end[-1:] // MOE_BLOCK).astype(jnp.int32)
    return dest, block_e, n_valid, n_used


def _layer(x, b0, mod, tabs, band, p):
    B, S, D = x.shape
    N = B * S
    E = p['w_g'].shape[0]
    h = _norm_mod(x, p['norm_mix_g'], mod, b0, 1, 0)
    q, k = _proj_qk(h, p['w_qk'], p['q_norm_g'], p['k_norm_g'], tabs)
    v, u, qn, kn, vn = _proj_mid(h, p['w_mid'])
    gates = _proj_gate(h.reshape(N, D), p['w_gate']).reshape(B, S, 3 * D)
    o_attn = _attention(q, k, v)
    o_pool = _pool(u, band, p['w_pool'], p['pool_scale'])
    o_na = _na(qn, kn, vn, p['na_bias'])
    x = _merge(o_attn, o_pool, o_na, gates, x, mod, b0, p['w_br_attn'], p['w_br_pool'], p['w_br_na'], p['w_out'])
    h2, te, tw, rk, cnt = _ffn_prep(x, p['norm_ffn_g'], mod, b0, p['wr_hi'], p['wr_lo'], p['br'])
    n_blocks = -(-N * TOP_K // MOE_BLOCK) + E
    dest, block_e, n_valid, n_used = _slot_tables(te[..., :TOP_K], rk[..., :TOP_K], cnt, n_blocks, E)
    xs = _dispatch(h2.reshape(N, D), dest, n_valid)
    act = _gm1(xs, block_e, n_valid, n_used, p['w_g'], p['w_u'], p['b_g'], p['b_u'])
    y_slots = _gm2(act, block_e, n_used, p['w_d'], p['b_d'])
    return _combine(x, tw, mod, b0, y_slots, dest)


def kernel(x_prompt, x_sample, c_prompt, c_sample, w_ada, b_ada, norm_mix_g, norm_ffn_g, w_in, q_norm_g, k_norm_g, w_pool, pool_scale, rpb, w_br_attn, w_br_pool, w_br_na, w_out, w_router, b_router, w_gu, b_gu, w_down, b_down, final_g):
    L, D, _ = w_ada.shape
    E = w_router.shape[-1]
    Fd = w_down.shape[2]
    Bp, Bs = c_prompt.shape[0], c_sample.shape[0]
    n_c = Bp + Bs
    c_all = jnp.zeros((-(-n_c // 8) * 8, D), F32).at[:n_c].set(jnp.concatenate([c_prompt, c_sample], axis=0))
    mod = _ada(c_all, w_ada, b_ada).reshape(L, c_all.shape[0], 6, 1, D)

    bias_idx, bias_valid = _na_bias_index()
    band = jnp.asarray(_pool_band(), BF16)
    qk_w = ATTN_W + KV_W
    mid_w = sum(MID_SECTIONS)
    wr = jnp.zeros((L, D, LANES), F32).at[:, :, :E].set(w_router)
    wr_hi = wr.astype(BF16)
    wr_lo = (wr - wr_hi.astype(F32)).astype(BF16)
    br = jnp.full((L, 1, LANES), NEG, F32).at[:, 0, :E].set(b_router)

    layers = []
    for l in range(L):
        layers.append(dict(
            norm_mix_g=norm_mix_g[l], norm_ffn_g=norm_ffn_g[l],
            q_norm_g=q_norm_g[l], k_norm_g=k_norm_g[l],
            w_qk=w_in[l, :, :qk_w].astype(BF16),
            w_mid=w_in[l, :, qk_w:qk_w + mid_w].astype(BF16),
            w_gate=w_in[l, :, qk_w + mid_w:].astype(BF16),
            w_pool=w_pool[l].astype(BF16), pool_scale=pool_scale[l],
            na_bias=jnp.where(bias_valid[None], rpb[l][:, bias_idx], NEG),
            w_br_attn=w_br_attn[l].astype(BF16), w_br_pool=w_br_pool[l].astype(BF16),
            w_br_na=w_br_na[l].astype(BF16), w_out=w_out[l].astype(BF16),
            wr_hi=wr_hi[l], wr_lo=wr_lo[l], br=br[l],
            w_g=w_gu[l, :, :, 0::2].astype(BF16), w_u=w_gu[l, :, :, 1::2].astype(BF16),
            b_g=b_gu[l, :, 0::2].reshape(E, 1, Fd), b_u=b_gu[l, :, 1::2].reshape(E, 1, Fd),
            w_d=w_down[l].astype(BF16), b_d=b_down[l].reshape(E, 1, D),
        ))

    outs = []
    for x, b0 in ((x_prompt, 0), (x_sample, Bp)):
        tabs = _rope_tables(x.shape[1])
        for l in range(L):
            x = _layer(x, b0, mod[l], tabs, band, layers[l])
        outs.append(_final_norm(x, final_g))
    return tuple(outs)
```

```python
import functools
import math

import numpy as np
import jax
import jax.numpy as jnp
from jax import lax
from jax.experimental import pallas as pl
from jax.experimental.pallas import tpu as pltpu

HEAD_DIM = 128
N_Q_HEADS = 8
N_KV_HEADS = 2
GQA_GROUP = N_Q_HEADS // N_KV_HEADS
ATTN_W = N_Q_HEADS * HEAD_DIM
KV_W = N_KV_HEADS * HEAD_DIM
GRID_W = 64
ROPE_THETA = 10000.0
ROPE_FREQS = HEAD_DIM // 4
POOL_WINDOWS = (2, 4, 8, 16)
N_POOL = len(POOL_WINDOWS)
POOL_DG = 128
POOL_W = N_POOL * POOL_DG
NA_HEADS = 4
NA_W = NA_HEADS * HEAD_DIM
NA_ROWS = 8
NA_COLS = 16
TOP_K = 4
MOE_BLOCK = 256
SWIGLU_ALPHA = 1.702
SWIGLU_LIMIT = 7.0
EPS = 1e-6

LANES = 128
POOL_HALO = 16
NEG = -1e30
VMEM_LIMIT = 56 * 1024 * 1024

F32 = jnp.float32
BF16 = jnp.bfloat16
_dot = functools.partial(jnp.dot, preferred_element_type=jnp.float32)


def _dot_nt(a, b):
    return lax.dot_general(a, b, (((1,), (1,)), ((), ())), preferred_element_type=jnp.float32)


def _cp(*sem, **kw):
    return pltpu.CompilerParams(dimension_semantics=sem, vmem_limit_bytes=VMEM_LIMIT, **kw)


def _split_bf16(a):
    hi = a.astype(BF16)
    lo = (a - hi.astype(F32)).astype(BF16)
    return hi, lo


def _resident(shape, index_map):
    return pl.BlockSpec(shape, index_map, pipeline_mode=pl.Buffered(1))


def _ada_kernel(c_ref, w_ref, b_ref, o_ref):
    c = c_ref[...]
    a = c * jax.nn.sigmoid(c)
    ah, al = _split_bf16(a)
    wh, wl = _split_bf16(w_ref[...])
    o_ref[...] = _dot(ah, wh) + _dot(al, wh) + _dot(ah, wl) + b_ref[...]


def _ada(c_all, w_ada, b_ada):
    L, D, D6 = w_ada.shape
    Bp = c_all.shape[0]
    tn = math.gcd(1024, D6)
    return pl.pallas_call(
        _ada_kernel,
        grid=(L, D6 // tn),
        in_specs=[pl.BlockSpec((Bp, D), lambda l, j: (0, 0)),
                  pl.BlockSpec((None, D, tn), lambda l, j: (l, 0, j)),
                  pl.BlockSpec((None, 1, tn), lambda l, j: (l, 0, j))],
        out_specs=pl.BlockSpec((None, Bp, tn), lambda l, j: (l, 0, j)),
        out_shape=jax.ShapeDtypeStruct((L, Bp, D6), F32),
        compiler_params=_cp("parallel", "parallel"),
        name="ada",
    )(c_all, w_ada, b_ada.reshape(L, 1, D6))


def _rms_mod(x, g, sc, sh):
    r = lax.rsqrt(jnp.mean(x * x, axis=-1, keepdims=True) + EPS)
    return (x * r * g) * (1.0 + sc) + sh


def _norm_mod_kernel(x_ref, g_ref, sc_ref, sh_ref, o_ref):
    o_ref[...] = _rms_mod(x_ref[...], g_ref[...], sc_ref[...], sh_ref[...]).astype(o_ref.dtype)


def _mod_spec(D, b0, which):
    return pl.BlockSpec((None, None, 1, D), lambda b, s: (b + b0, which, 0, 0))


def _norm_mod(x, g, mod, b0, i_sc, i_sh):
    B, S, D = x.shape
    ts = min(512, S)
    return pl.pallas_call(
        _norm_mod_kernel,
        grid=(B, S // ts),
        in_specs=[pl.BlockSpec((None, ts, D), lambda b, s: (b, s, 0)),
                  pl.BlockSpec((1, D), lambda b, s: (0, 0)),
                  _mod_spec(D, b0, i_sc), _mod_spec(D, b0, i_sh)],
        out_specs=pl.BlockSpec((None, ts, D), lambda b, s: (b, s, 0)),
        out_shape=jax.ShapeDtypeStruct((B, S, D), BF16),
        compiler_params=_cp("parallel", "parallel"),
        name="norm_mod",
    )(x, g.reshape(1, D), mod, mod)


def _final_norm_kernel(x_ref, g_ref, o_ref):
    x = x_ref[...]
    r = lax.rsqrt(jnp.mean(x * x, axis=-1, keepdims=True) + EPS)
    o_ref[...] = x * r * g_ref[...]


def _final_norm(x, g):
    B, S, D = x.shape
    ts = min(512, S)
    return pl.pallas_call(
        _final_norm_kernel,
        grid=(B, S // ts),
        in_specs=[pl.BlockSpec((None, ts, D), lambda b, s: (b, s, 0)),
                  pl.BlockSpec((1, D), lambda b, s: (0, 0))],
        out_specs=pl.BlockSpec((None, ts, D), lambda b, s: (b, s, 0)),
        out_shape=jax.ShapeDtypeStruct((B, S, D), F32),
        compiler_params=_cp("parallel", "parallel"),
        name="final_norm",
    )(x, g.reshape(1, D))


def _rope_tables(S):
    t = jnp.arange(S)
    pos = jnp.stack([t // GRID_W, t % GRID_W], axis=-1).astype(F32)
    inv = ROPE_THETA ** (-jnp.arange(ROPE_FREQS, dtype=F32) / ROPE_FREQS)
    ang = pos[..., None] * inv
    cos, sin = jnp.cos(ang), jnp.sin(ang)
    zero = jnp.zeros_like(sin)
    c = jnp.stack([cos, cos], axis=2).reshape(S, HEAD_DIM)
    sa = jnp.stack([-sin, zero], axis=2).reshape(S, HEAD_DIM)
    sb = jnp.stack([zero, sin], axis=2).reshape(S, HEAD_DIM)
    return c, sa, sb


def _proj_qk_kernel(a_ref, w_ref, gq_ref, gk_ref, c_ref, sa_ref, sb_ref, q_ref, k_ref):
    acc = _dot(a_ref[...], w_ref[...])
    c, sa, sb = c_ref[...], sa_ref[...], sb_ref[...]
    scale = 1.0 / math.sqrt(HEAD_DIM)
    for hh in range(N_Q_HEADS + N_KV_HEADS):
        xh = acc[:, hh * HEAD_DIM:(hh + 1) * HEAD_DIM]
        r = lax.rsqrt(jnp.mean(xh * xh, axis=-1, keepdims=True) + EPS)
        is_q = hh < N_Q_HEADS
        xn = xh * r * (gq_ref[...] if is_q else gk_ref[...])
        y = xn * c + pltpu.roll(xn, HEAD_DIM - ROPE_FREQS, 1) * sa + pltpu.roll(xn, ROPE_FREQS, 1) * sb
        if is_q:
            q_ref[:, hh * HEAD_DIM:(hh + 1) * HEAD_DIM] = (y * scale).astype(q_ref.dtype)
        else:
            kk = hh - N_Q_HEADS
            k_ref[:, kk * HEAD_DIM:(kk + 1) * HEAD_DIM] = y.astype(k_ref.dtype)


def _proj_qk(h, w_qk, gq, gk, tabs):
    B, S, D = h.shape
    tm = min(1024, S)
    W = ATTN_W + KV_W
    tab_spec = pl.BlockSpec((tm, HEAD_DIM), lambda b, s: (s, 0))
    vec_spec = pl.BlockSpec((1, HEAD_DIM), lambda b, s: (0, 0))
    return pl.pallas_call(
        _proj_qk_kernel,
        grid=(B, S // tm),
        in_specs=[pl.BlockSpec((None, tm, D), lambda b, s: (b, s, 0)),
                  _resident((D, W), lambda b, s: (0, 0)),
                  vec_spec, vec_spec, tab_spec, tab_spec, tab_spec],
        out_specs=[pl.BlockSpec((None, tm, ATTN_W), lambda b, s: (b, s, 0)),
                   pl.BlockSpec((None, tm, KV_W), lambda b, s: (b, s, 0))],
        out_shape=[jax.ShapeDtypeStruct((B, S, ATTN_W), BF16),
                   jax.ShapeDtypeStruct((B, S, KV_W), BF16)],
        compiler_params=_cp("parallel", "parallel"),
        name="proj_qk",
    )(h, w_qk, gq.reshape(1, HEAD_DIM), gk.reshape(1, HEAD_DIM), *tabs)


MID_SECTIONS = (POOL_W, NA_W, NA_W, NA_W)
ATTN_TK = 512


def _proj_mid_kernel(a_ref, w_ref, wvt_ref, vt_ref, *o_refs):
    a = a_ref[...]
    acc = _dot(a, w_ref[...])
    off = 0
    for o_ref, width in zip(o_refs, MID_SECTIONS):
        o_ref[...] = acc[:, off:off + width].astype(o_ref.dtype)
        off += width
    tk = vt_ref.shape[-1]
    for c in range(vt_ref.shape[0]):
        vt_ref[c] = _dot_nt(wvt_ref[...], a[c * tk:(c + 1) * tk]).astype(vt_ref.dtype)


def _proj_mid(h, w_mid, w_vt):
    B, S, D = h.shape
    tm = min(1024, S)
    tk = min(ATTN_TK, S)
    W = sum(MID_SECTIONS)
    return pl.pallas_call(
        _proj_mid_kernel,
        grid=(B, S // tm),
        in_specs=[pl.BlockSpec((None, tm, D), lambda b, s: (b, s, 0)),
                  _resident((D, W), lambda b, s: (0, 0)),
                  _resident((KV_W, D), lambda b, s: (0, 0))],
        out_specs=[pl.BlockSpec((None, tm // tk, KV_W, tk), lambda b, s: (b, s, 0, 0))]
        + [pl.BlockSpec((None, tm, w), lambda b, s: (b, s, 0)) for w in MID_SECTIONS],
        out_shape=[jax.ShapeDtypeStruct((B, S // tk, KV_W, tk), BF16)]
        + [jax.ShapeDtypeStruct((B, S, w), BF16) for w in MID_SECTIONS],
        compiler_params=_cp("parallel", "parallel"),
        name="proj_mid",
    )(h, w_mid, w_vt)


def _proj_gate_kernel(a_ref, w_ref, o_ref):
    o_ref[...] = jax.nn.sigmoid(_dot(a_ref[...], w_ref[...])).astype(o_ref.dtype)


def _proj_gate(h, w_gate):
    N, D = h.shape
    W = w_gate.shape[1]
    tm = min(1024, N)
    tn = min(2048, D)
    return pl.pallas_call(
        _proj_gate_kernel,
        grid=(W // tn, N // tm),
        in_specs=[pl.BlockSpec((tm, D), lambda j, i: (i, 0)),
                  pl.BlockSpec((D, tn), lambda j, i: (0, j))],
        out_specs=pl.BlockSpec((tm, tn), lambda j, i: (i, j)),
        out_shape=jax.ShapeDtypeStruct((N, W), BF16),
        compiler_params=_cp("parallel", "parallel"),
        name="proj_gate",
    )(h, w_gate)


def _attn_kernel(q_ref, k_ref, vt_ref, o_ref):
    n_chunks, _, tk = vt_ref.shape
    tq = q_ref.shape[0]
    heads = range(GQA_GROUP)

    def body(j, carry):
        k = k_ref[pl.ds(pl.multiple_of(j * tk, tk), tk), :]
        vt = vt_ref[j]
        out = []
        for h in heads:
            m, l, acc = carry[h]
            s = _dot_nt(k, q_ref[:, h * HEAD_DIM:(h + 1) * HEAD_DIM])
            m_new = jnp.maximum(m, jnp.max(s, axis=0, keepdims=True))
            alpha = jnp.exp(m - m_new)
            p = jnp.exp(s - m_new)
            l = alpha * l + jnp.sum(p, axis=0, keepdims=True)
            acc = alpha * acc + _dot(vt, p.astype(BF16))
            out.append((m_new, l, acc))
        return tuple(out)

    init = tuple((jnp.full((1, tq), NEG, F32), jnp.zeros((1, tq), F32), jnp.zeros((HEAD_DIM, tq), F32))
                 for _ in heads)
    final = lax.fori_loop(0, n_chunks, body, init)
    for h in heads:
        m, l, acc = final[h]
        o_ref[:, h * HEAD_DIM:(h + 1) * HEAD_DIM] = (acc / l).T.astype(o_ref.dtype)


def _attention(q, k, vt):
    B, S, _ = q.shape
    n_chunks, _, tk = vt.shape[1:]
    tq = min(512, S)
    gw = GQA_GROUP * HEAD_DIM
    return pl.pallas_call(
        _attn_kernel,
        grid=(B, N_KV_HEADS, S // tq),
        in_specs=[pl.BlockSpec((None, tq, gw), lambda b, g, i: (b, i, g)),
                  pl.BlockSpec((None, S, HEAD_DIM), lambda b, g, i: (b, 0, g)),
                  pl.BlockSpec((None, n_chunks, HEAD_DIM, tk), lambda b, g, i: (b, 0, g, 0))],
        out_specs=pl.BlockSpec((None, tq, gw), lambda b, g, i: (b, i, g)),
        out_shape=jax.ShapeDtypeStruct((B, S, ATTN_W), BF16),
        compiler_params=_cp("parallel", "parallel", "parallel"),
        name="gqa",
    )(q, k, vt)


def _pool_band():
    T = MOE_BLOCK
    i = np.arange(T)[:, None] + POOL_HALO
    j = np.arange(T + 2 * POOL_HALO)[None, :]
    return np.stack([((j >= i - w // 2) & (j < i + w // 2)) for w in POOL_WINDOWS]).astype(np.float32)


def _pool_kernel(u_ref, band_ref, wp_ref, ps_ref, o_ref, *, T):
    S = u_ref.shape[0]
    nc = S // T
    halo = jnp.zeros((POOL_HALO, POOL_DG), BF16)
    for g in range(N_POOL):
        hw = POOL_WINDOWS[g] // 2
        cols = slice(g * POOL_DG, (g + 1) * POOL_DG)
        for c in range(nc):
            t0 = c * T
            mid = u_ref[t0:t0 + T, cols]
            prev = u_ref[t0 - POOL_HALO:t0, cols] if c > 0 else halo
            nxt = u_ref[t0 + T:t0 + T + POOL_HALO, cols] if c < nc - 1 else halo
            win = _dot(band_ref[g], jnp.concatenate([prev, mid, nxt], axis=0))
            t = t0 + lax.broadcasted_iota(jnp.int32, (T, POOL_DG), 0)
            cnt = (jnp.minimum(t + hw, S) - jnp.maximum(t - hw, 0)).astype(F32)
            pooled = win / cnt - mid.astype(F32)
            mixed = _dot(pooled.astype(BF16), wp_ref[g]) * ps_ref[:, cols]
            o_ref[t0:t0 + T, cols] = mixed.astype(o_ref.dtype)


def _pool(u, band, w_pool, pool_scale):
    B, S, _ = u.shape
    T = band.shape[1]
    return pl.pallas_call(
        functools.partial(_pool_kernel, T=T),
        grid=(B,),
        in_specs=[pl.BlockSpec((None, S, POOL_W), lambda b: (b, 0, 0)),
                  pl.BlockSpec(band.shape, lambda b: (0, 0, 0)),
                  pl.BlockSpec(w_pool.shape, lambda b: (0, 0, 0)),
                  pl.BlockSpec((1, POOL_W), lambda b: (0, 0))],
        out_specs=pl.BlockSpec((None, S, POOL_W), lambda b: (b, 0, 0)),
        out_shape=jax.ShapeDtypeStruct((B, S, POOL_W), BF16),
        compiler_params=_cp("parallel"),
        name="pool",
    )(u, band, w_pool, pool_scale.reshape(1, POOL_W))


def _na_bias(rpb):
    H = rpb.shape[0]
    nr, nc, P = 2 * NA_ROWS - 1, 2 * NA_COLS - 1, 2 * GRID_W
    w = jnp.full((H, nr, P), NEG, F32).at[:, :, GRID_W - NA_COLS:GRID_W - 1 + NA_COLS].set(rpb.reshape(H, nr, nc))
    skew = jnp.tile(w, (1, 1, GRID_W))[:, :, :GRID_W * (P - 1)].reshape(H, nr, GRID_W, P - 1)
    toe = skew[..., GRID_W - 1:2 * GRID_W - 1]
    qc = np.arange(GRID_W)[:, None]
    kc = np.arange(GRID_W)[None, :]
    cs = np.clip(qc - NA_COLS // 2, 0, GRID_W - NA_COLS)
    toe = jnp.where((kc >= cs) & (kc < cs + NA_COLS), toe, NEG)
    per_off = jnp.stack([toe[:, NA_ROWS - 1 - o:2 * NA_ROWS - 1 - o] for o in range(NA_ROWS)], axis=1)
    return per_off.transpose(0, 1, 3, 2, 4).reshape(H, NA_ROWS, GRID_W, NA_ROWS * GRID_W)


def _na_kernel(q_ref, k_ref, v_ref, b_ref, o_ref, *, rb, R):
    i = pl.program_id(1)
    scale = 1.0 / math.sqrt(HEAD_DIM)
    win = NA_ROWS * GRID_W
    for rr in range(rb):
        r = i * rb + rr
        rs = jnp.clip(r - NA_ROWS // 2, 0, R - NA_ROWS)
        off = r - rs
        start = pl.multiple_of(rs * GRID_W, GRID_W)
        rows = slice(rr * GRID_W, (rr + 1) * GRID_W)
        for h in range(NA_HEADS):
            cols = slice(h * HEAD_DIM, (h + 1) * HEAD_DIM)
            s = _dot_nt(q_ref[rows, cols], k_ref[pl.ds(start, win), cols]) * scale + b_ref[h, off]
            m = jnp.max(s, axis=-1, keepdims=True)
            p = jnp.exp(s - m)
            l = jnp.sum(p, axis=-1, keepdims=True)
            out = _dot(p.astype(BF16), v_ref[pl.ds(start, win), cols]) / l
            o_ref[rows, cols] = out.astype(o_ref.dtype)


def _na(q, k, v, bias):
    B, S, _ = q.shape
    R = S // GRID_W
    assert R >= NA_ROWS
    rb = 8
    return pl.pallas_call(
        functools.partial(_na_kernel, rb=rb, R=R),
        grid=(B, R // rb),
        in_specs=[pl.BlockSpec((None, rb * GRID_W, NA_W), lambda b, i: (b, i, 0)),
                  pl.BlockSpec((None, S, NA_W), lambda b, i: (b, 0, 0)),
                  pl.BlockSpec((None, S, NA_W), lambda b, i: (b, 0, 0)),
                  pl.BlockSpec(bias.shape, lambda b, i: (0, 0, 0, 0))],
        out_specs=pl.BlockSpec((None, rb * GRID_W, NA_W), lambda b, i: (b, i, 0)),
        out_shape=jax.ShapeDtypeStruct((B, S, NA_W), BF16),
        compiler_params=_cp("parallel", "parallel"),
        name="natten",
    )(q, k, v, bias)


def _merge_kernel(oa_ref, op_ref, on_ref, ga_ref, gp_ref, gn_ref, x_ref, g1_ref,
                  wa_ref, wp_ref, wn_ref, wo_ref, o_ref):
    m = ga_ref[...].astype(F32) * _dot(oa_ref[...], wa_ref[...])
    m = m + gp_ref[...].astype(F32) * _dot(op_ref[...], wp_ref[...])
    m = m + gn_ref[...].astype(F32) * _dot(on_ref[...], wn_ref[...])
    y = _dot(m.astype(BF16), wo_ref[...])
    o_ref[...] = x_ref[...] + g1_ref[...] * y


def _merge(o_attn, o_pool, o_na, gates, x, mod, b0, w_a, w_p, w_n, w_o):
    B, S, D = x.shape
    tm = min(512, S)
    row = lambda w: pl.BlockSpec((None, tm, w), lambda b, s: (b, s, 0))
    gate = lambda j: pl.BlockSpec((None, tm, D), lambda b, s: (b, s, j))
    res = lambda w: _resident(w.shape, lambda b, s: (0, 0))
    return pl.pallas_call(
        _merge_kernel,
        grid=(B, S // tm),
        in_specs=[row(ATTN_W), row(POOL_W), row(NA_W), gate(0), gate(1), gate(2), row(D),
                  _mod_spec(D, b0, 2), res(w_a), res(w_p), res(w_n), res(w_o)],
        out_specs=row(D),
        out_shape=jax.ShapeDtypeStruct((B, S, D), F32),
        compiler_params=_cp("parallel", "parallel"),
        name="merge",
    )(o_attn, o_pool, o_na, gates, gates, gates, x, mod, w_a, w_p, w_n, w_o)


def _ffn_prep_kernel(x_ref, g_ref, sc_ref, sh_ref, wrh_ref, wrl_ref, br_ref,
                     h_ref, te_ref, tw_ref, rk_ref, cnt_ref, carry_ref):
    first = jnp.logical_and(pl.program_id(0) == 0, pl.program_id(1) == 0)

    @pl.when(first)
    def _():
        carry_ref[...] = jnp.zeros_like(carry_ref)

    tm = x_ref.shape[0]
    h = _rms_mod(x_ref[...], g_ref[...], sc_ref[...], sh_ref[...])
    h_ref[...] = h
    hh, hl = _split_bf16(h)
    wrh = wrh_ref[...]
    logits = _dot(hh, wrh) + _dot(hl, wrh) + _dot(hh, wrl_ref[...]) + br_ref[...]
    lane = lax.broadcasted_iota(jnp.int32, (tm, LANES), 1).astype(F32)
    vals, idxs, hots = [], [], []
    cur = logits
    for _ in range(TOP_K):
        m = jnp.max(cur, axis=-1, keepdims=True)
        idx = jnp.min(jnp.where(cur == m, lane, float(LANES)), axis=-1, keepdims=True)
        hot = lane == idx
        vals.append(m)
        idxs.append(idx)
        hots.append(hot)
        cur = jnp.where(hot, -3e38, cur)
    ex = [jnp.exp(v - vals[0]) for v in vals]
    den = ex[0] + ex[1] + ex[2] + ex[3]
    onehot = sum(hh_.astype(F32) for hh_ in hots)
    ri = lax.broadcasted_iota(jnp.int32, (tm, tm), 0)
    ci = lax.broadcasted_iota(jnp.int32, (tm, tm), 1)
    tri = (ri > ci).astype(BF16)
    carry = carry_ref[0:1, :]
    before = _dot(tri, onehot.astype(BF16)) + carry
    te = jnp.zeros((tm, LANES), F32)
    tw = jnp.zeros((tm, LANES), F32)
    rk = jnp.zeros((tm, LANES), F32)
    for k in range(TOP_K):
        sel = lane == float(k)
        rank = jnp.sum(jnp.where(hots[k], before, 0.0), axis=-1, keepdims=True)
        te = jnp.where(sel, idxs[k], te)
        tw = jnp.where(sel, ex[k] / den, tw)
        rk = jnp.where(sel, rank, rk)
    te_ref[...] = te.astype(jnp.int32)
    tw_ref[...] = tw
    rk_ref[...] = rk.astype(jnp.int32)
    total = carry + jnp.sum(onehot, axis=0, keepdims=True)
    carry_ref[...] = jnp.broadcast_to(total, carry_ref.shape)
    cnt_ref[...] = jnp.broadcast_to(total, cnt_ref.shape).astype(jnp.int32)


def _ffn_prep(x, g, mod, b0, wr_hi, wr_lo, br):
    B, S, D = x.shape
    tm = min(512, S)
    row = lambda w: pl.BlockSpec((None, tm, w), lambda b, s: (b, s, 0))
    const = lambda shape: pl.BlockSpec(shape, lambda b, s: (0, 0))
    return pl.pallas_call(
        _ffn_prep_kernel,
        grid=(B, S // tm),
        in_specs=[row(D), const((1, D)), _mod_spec(D, b0, 4), _mod_spec(D, b0, 3),
                  const((D, LANES)), const((D, LANES)), const((1, LANES))],
        out_specs=[row(D), row(LANES), row(LANES), row(LANES), const((8, LANES))],
        out_shape=[jax.ShapeDtypeStruct((B, S, D), F32),
                   jax.ShapeDtypeStruct((B, S, LANES), jnp.int32),
                   jax.ShapeDtypeStruct((B, S, LANES), F32),
                   jax.ShapeDtypeStruct((B, S, LANES), jnp.int32),
                   jax.ShapeDtypeStruct((8, LANES), jnp.int32)],
        scratch_shapes=[pltpu.VMEM((8, LANES), F32)],
        compiler_params=_cp("arbitrary", "arbitrary"),
        name="ffn_prep",
    )(x, g.reshape(1, D), mod, mod, wr_hi, wr_lo, br)


def _dispatch_kernel(nv_ref, dest_ref, h_ref, xs_ref, zeros, sem, zsem):
    tm = h_ref.shape[0]
    nb = nv_ref.shape[0]

    @pl.when(pl.program_id(0) == 0)
    def _():
        zeros[...] = jnp.zeros_like(zeros)

        def tail_copies(b, fn):
            nv = nv_ref[b]
            base = b * MOE_BLOCK
            up = (nv + 7) // 8 * 8
            for r in range(7):
                @pl.when(nv + r < up)
                def _(r=r):
                    fn(pltpu.make_async_copy(zeros.at[pl.ds(0, 1), :], xs_ref.at[pl.ds(base + nv + r, 1), :], zsem))
            units = (MOE_BLOCK - up) // 8
            off = up
            for bit in (32, 16, 8, 4, 2, 1):
                size = bit * 8
                hit = (units & bit) != 0

                @pl.when(hit)
                def _(size=size, off=off):
                    dst = xs_ref.at[pl.ds(pl.multiple_of(base + off, 8), size), :]
                    fn(pltpu.make_async_copy(zeros.at[pl.ds(0, size), :], dst, zsem))
                off = off + jnp.where(hit, size, 0)

        def start_all(b, c):
            tail_copies(b, lambda cp: cp.start())
            return c

        def wait_all(b, c):
            tail_copies(b, lambda cp: cp.wait())
            return c

        lax.fori_loop(0, nb, start_all, 0)
        lax.fori_loop(0, nb, wait_all, 0)

    def row_copy(r, d):
        return pltpu.make_async_copy(h_ref.at[pl.ds(r, 1), :], xs_ref.at[pl.ds(d, 1), :], sem)

    def issue(r, c):
        for k in range(TOP_K):
            row_copy(r, dest_ref[r * TOP_K + k]).start()
        return c

    lax.fori_loop(0, tm, issue, 0)

    def drain(r, c):
        row_copy(0, 0).wait()
        return c

    lax.fori_loop(0, tm * TOP_K, drain, 0, unroll=8)


def _dispatch(h, dest, n_valid):
    N, D = h.shape
    tm = min(512, N)
    n_slots = n_valid.shape[0] * MOE_BLOCK
    return pl.pallas_call(
        _dispatch_kernel,
        grid_spec=pltpu.PrefetchScalarGridSpec(
            num_scalar_prefetch=1, grid=(N // tm,),
            in_specs=[pl.BlockSpec((tm * TOP_K,), lambda i, nv: (i,), memory_space=pltpu.SMEM),
                      pl.BlockSpec((tm, D), lambda i, nv: (i, 0))],
            out_specs=pl.BlockSpec(memory_space=pl.ANY),
            scratch_shapes=[pltpu.VMEM((MOE_BLOCK, D), F32), pltpu.SemaphoreType.DMA(()),
                            pltpu.SemaphoreType.DMA(())]),
        out_shape=jax.ShapeDtypeStruct((n_slots, D), F32),
        compiler_params=_cp("arbitrary", has_side_effects=True),
        name="dispatch",
    )(n_valid, dest, h)


SPLIT_CHUNK = 2 * LANES


def _split_gu_kernel(w_ref, sel_ref, g_ref, u_ref):
    sel = sel_ref[...]
    for c in range(w_ref.shape[1] // SPLIT_CHUNK):
        chunk = w_ref[:, c * SPLIT_CHUNK:(c + 1) * SPLIT_CHUNK].astype(BF16)
        r = _dot(chunk, sel)
        g_ref[:, c * LANES:(c + 1) * LANES] = r[:, :LANES].astype(g_ref.dtype)
        u_ref[:, c * LANES:(c + 1) * LANES] = r[:, LANES:].astype(u_ref.dtype)


def _split_gu(w_gu):
    L, E, D, F2 = w_gu.shape
    Fd = F2 // 2
    tk = min(256, D)
    src = np.arange(SPLIT_CHUNK)[:, None]
    dst = np.arange(SPLIT_CHUNK)[None, :]
    sel = jnp.asarray(src == 2 * (dst % LANES) + dst // LANES, BF16)
    out_spec = pl.BlockSpec((None, None, tk, Fd), lambda l, e, k: (l, e, k, 0))
    return pl.pallas_call(
        _split_gu_kernel,
        grid=(L, E, D // tk),
        in_specs=[pl.BlockSpec((None, None, tk, F2), lambda l, e, k: (l, e, k, 0)),
                  pl.BlockSpec((SPLIT_CHUNK, SPLIT_CHUNK), lambda l, e, k: (0, 0))],
        out_specs=[out_spec, out_spec],
        out_shape=[jax.ShapeDtypeStruct((L, E, D, Fd), BF16)] * 2,
        compiler_params=_cp("parallel", "parallel", "parallel"),
        name="split_gate_up",
    )(w_gu, sel)


def _gm1_kernel(be_ref, nv_ref, nu_ref, x_ref, wg_ref, wu_ref, bg_ref, bu_ref, o_ref):
    i = pl.program_id(0)

    @pl.when(i < nu_ref[0])
    def _():
        rows = lax.broadcasted_iota(jnp.int32, x_ref.shape, 0)
        x = jnp.where(rows < nv_ref[i], x_ref[...], 0.0).astype(BF16)
        g = _dot(x, wg_ref[...]) + bg_ref[...]
        u = _dot(x, wu_ref[...]) + bu_ref[...]
        gate = jnp.minimum(g, SWIGLU_LIMIT)
        up = jnp.clip(u, -SWIGLU_LIMIT, SWIGLU_LIMIT)
        o_ref[...] = ((up + 1.0) * gate * jax.nn.sigmoid(SWIGLU_ALPHA * gate)).astype(o_ref.dtype)

    @pl.when(i >= nu_ref[0])
    def _():
        o_ref[...] = jnp.zeros_like(o_ref)


def _gm1(xs, block_e, n_valid, n_used, layer, w_g, w_u, b_g, b_u):
    n_slots, D = xs.shape
    Fd = w_g.shape[-1]
    nb = n_slots // MOE_BLOCK
    wspec = pl.BlockSpec((None, None, D, Fd), lambda i, be, nv, nu: (layer, be[i], 0, 0))
    bspec = pl.BlockSpec((None, 1, Fd), lambda i, be, nv, nu: (be[i], 0, 0))
    return pl.pallas_call(
        _gm1_kernel,
        grid_spec=pltpu.PrefetchScalarGridSpec(
            num_scalar_prefetch=3, grid=(nb,),
            in_specs=[pl.BlockSpec((MOE_BLOCK, D), lambda i, be, nv, nu: (i, 0)), wspec, wspec, bspec, bspec],
            out_specs=pl.BlockSpec((MOE_BLOCK, Fd), lambda i, be, nv, nu: (i, 0))),
        out_shape=jax.ShapeDtypeStruct((n_slots, Fd), BF16),
        compiler_params=_cp("arbitrary"),
        name="expert_up",
    )(block_e, n_valid, n_used, xs, w_g, w_u, b_g, b_u)


def _gm2_kernel(be_ref, nu_ref, a_ref, wd_ref, bd_ref, o_ref):
    i = pl.program_id(0)

    @pl.when(i < nu_ref[0])
    def _():
        o_ref[...] = _dot(a_ref[...], wd_ref[...]) + bd_ref[...]

    @pl.when(i >= nu_ref[0])
    def _():
        o_ref[...] = jnp.zeros_like(o_ref)


def _gm2(act, block_e, n_used, layer, w_d, b_d):
    n_slots, Fd = act.shape
    D = w_d.shape[-1]
    nb = n_slots // MOE_BLOCK
    return pl.pallas_call(
        _gm2_kernel,
        grid_spec=pltpu.PrefetchScalarGridSpec(
            num_scalar_prefetch=2, grid=(nb,),
            in_specs=[pl.BlockSpec((MOE_BLOCK, Fd), lambda i, be, nu: (i, 0)),
                      pl.BlockSpec((None, None, Fd, D), lambda i, be, nu: (layer, be[i], 0, 0)),
                      pl.BlockSpec((None, 1, D), lambda i, be, nu: (be[i], 0, 0))],
            out_specs=pl.BlockSpec((MOE_BLOCK, D), lambda i, be, nu: (i, 0))),
        out_shape=jax.ShapeDtypeStruct((n_slots, D), F32),
        compiler_params=_cp("arbitrary"),
        name="expert_down",
    )(block_e, n_used, act, w_d, b_d)


def _combine_kernel(dest_ref, x_ref, tw_ref, g2_ref, y_ref, o_ref, buf, sem):
    tm = x_ref.shape[0]

    def row_copy(r, k, d):
        return pltpu.make_async_copy(y_ref.at[pl.ds(d, 1), :], buf.at[k, pl.ds(r, 1), :], sem)

    def issue(r, c):
        for k in range(TOP_K):
            row_copy(r, k, dest_ref[r * TOP_K + k]).start()
        return c

    lax.fori_loop(0, tm, issue, 0)

    def drain(r, c):
        row_copy(0, 0, 0).wait()
        return c

    lax.fori_loop(0, tm * TOP_K, drain, 0, unroll=8)
    tw = tw_ref[...]
    acc = tw[:, 0:1] * buf[0]
    for k in range(1, TOP_K):
        acc = acc + tw[:, k:k + 1] * buf[k]
    o_ref[...] = x_ref[...] + g2_ref[...] * acc


def _combine(x, tw, mod, b0, y_slots, dest):
    B, S, D = x.shape
    tm = min(256, S)
    spb = S // tm
    row = lambda w: pl.BlockSpec((None, tm, w), lambda b, s: (b, s, 0))
    return pl.pallas_call(
        _combine_kernel,
        grid=(B, spb),
        in_specs=[pl.BlockSpec((tm * TOP_K,), lambda b, s: (b * spb + s,), memory_space=pltpu.SMEM),
                  row(D), row(LANES), _mod_spec(D, b0, 5),
                  pl.BlockSpec(memory_space=pl.ANY)],
        out_specs=row(D),
        out_shape=jax.ShapeDtypeStruct((B, S, D), F32),
        scratch_shapes=[pltpu.VMEM((TOP_K, tm, D), F32), pltpu.SemaphoreType.DMA(())],
        compiler_params=_cp("arbitrary", "arbitrary"),
        name="combine",
    )(dest, x, tw, mod, y_slots)


def _slot_tables(te, rk, cnt, n_blocks, E):
    counts = cnt[0, :E]
    padded = (counts + MOE_BLOCK - 1) // MOE_BLOCK * MOE_BLOCK
    pad_end = jnp.cumsum(padded)
    pad_start = pad_end - padded
    experts = jnp.arange(E, dtype=jnp.int32)

    def lookup(table, idx):
        return jnp.sum(jnp.where(idx[..., None] == experts, table, 0), axis=-1)

    dest = (lookup(pad_start, te) + rk).reshape(-1).astype(jnp.int32)
    blk0 = jnp.arange(n_blocks, dtype=jnp.int32) * MOE_BLOCK
    block_e = jnp.minimum(jnp.sum(pad_end[None, :] <= blk0[:, None], axis=1), E - 1).astype(jnp.int32)
    n_valid = jnp.clip(lookup(counts, block_e) - (blk0 - lookup(pad_start, block_e)), 0, MOE_BLOCK).astype(jnp.int32)
    n_used = (pad_end[-1:] // MOE_BLOCK).astype(jnp.int32)
    return dest, block_e, n_valid, n_used


def _layer(x, b0, mod, tabs, band, p):
    B, S, D = x.shape
    N = B * S
    E = p['w_g'].shape[1]
    h = _norm_mod(x, p['norm_mix_g'], mod, b0, 1, 0)
    q, k = _proj_qk(h, p['w_qk'], p['q_norm_g'], p['k_norm_g'], tabs)
    vt, u, qn, kn, vn = _proj_mid(h, p['w_mid'], p['w_vt'])
    gates = _proj_gate(h.reshape(N, D), p['w_gate']).reshape(B, S, 3 * D)
    o_attn = _attention(q, k, vt)
    o_pool = _pool(u, band, p['w_pool'], p['pool_scale'])
    o_na = _na(qn, kn, vn, p['na_bias'])
    x = _merge(o_attn, o_pool, o_na, gates, x, mod, b0, p['w_br_attn'], p['w_br_pool'], p['w_br_na'], p['w_out'])
    h2, te, tw, rk, cnt = _ffn_prep(x, p['norm_ffn_g'], mod, b0, p['wr_hi'], p['wr_lo'], p['br'])
    n_blocks = -(-N * TOP_K // MOE_BLOCK) + E
    dest, block_e, n_valid, n_used = _slot_tables(te[..., :TOP_K], rk[..., :TOP_K], cnt, n_blocks, E)
    xs = _dispatch(h2.reshape(N, D), dest, n_valid)
    act = _gm1(xs, block_e, n_valid, n_used, p['layer'], p['w_g'], p['w_u'], p['b_g'], p['b_u'])
    y_slots = _gm2(act, block_e, n_used, p['layer'], p['w_d'], p['b_d'])
    return _combine(x, tw, mod, b0, y_slots, dest)


def kernel(x_prompt, x_sample, c_prompt, c_sample, w_ada, b_ada, norm_mix_g, norm_ffn_g, w_in, q_norm_g, k_norm_g, w_pool, pool_scale, rpb, w_br_attn, w_br_pool, w_br_na, w_out, w_router, b_router, w_gu, b_gu, w_down, b_down, final_g):
    L, D, _ = w_ada.shape
    E = w_router.shape[-1]
    Fd = w_down.shape[2]
    Bp, Bs = c_prompt.shape[0], c_sample.shape[0]
    n_c = Bp + Bs
    c_all = jnp.zeros((-(-n_c // 8) * 8, D), F32).at[:n_c].set(jnp.concatenate([c_prompt, c_sample], axis=0))
    mod = _ada(c_all, w_ada, b_ada).reshape(L, c_all.shape[0], 6, 1, D)

    band = jnp.asarray(_pool_band(), BF16)
    w_g, w_u = _split_gu(w_gu)
    w_d = w_down.astype(BF16)
    b_gu2 = b_gu.reshape(L, E, Fd, 2)
    qk_w = ATTN_W + KV_W
    mid_w = sum(MID_SECTIONS)
    wr = jnp.zeros((L, D, LANES), F32).at[:, :, :E].set(w_router)
    wr_hi = wr.astype(BF16)
    wr_lo = (wr - wr_hi.astype(F32)).astype(BF16)
    br = jnp.full((L, 1, LANES), NEG, F32).at[:, 0, :E].set(b_router)

    layers = []
    for l in range(L):
        layers.append(dict(
            norm_mix_g=norm_mix_g[l], norm_ffn_g=norm_ffn_g[l],
            q_norm_g=q_norm_g[l], k_norm_g=k_norm_g[l],
            w_qk=w_in[l, :, :qk_w].astype(BF16),
            w_vt=w_in[l, :, qk_w:qk_w + KV_W].T.astype(BF16),
            w_mid=w_in[l, :, qk_w + KV_W:qk_w + KV_W + mid_w].astype(BF16),
            w_gate=w_in[l, :, qk_w + KV_W + mid_w:].astype(BF16),
            w_pool=w_pool[l].astype(BF16), pool_scale=pool_scale[l],
            na_bias=_na_bias(rpb[l]),
            w_br_attn=w_br_attn[l].astype(BF16), w_br_pool=w_br_pool[l].astype(BF16),
            w_br_na=w_br_na[l].astype(BF16), w_out=w_out[l].astype(BF16),
            wr_hi=wr_hi[l], wr_lo=wr_lo[l], br=br[l],
            layer=l, w_g=w_g, w_u=w_u,
            b_g=b_gu2[l, :, :, 0].reshape(E, 1, Fd), b_u=b_gu2[l, :, :, 1].reshape(E, 1, Fd),
            w_d=w_d, b_d=b_down[l].reshape(E, 1, D),
        ))

    outs = []
    for x, b0 in ((x_prompt, 0), (x_sample, Bp)):
        tabs = _rope_tables(x.shape[1])
        for l in range(L):
            x = _layer(x, b0, mod[l], tabs, band, layers[l])
        outs.append(_final_norm(x, final_g))
    return tuple(outs)
```

```python
import functools
import math

import numpy as np
import jax
import jax.numpy as jnp
from jax import lax
from jax.experimental import pallas as pl
from jax.experimental.pallas import tpu as pltpu

HEAD_DIM = 128
N_Q_HEADS = 8
N_KV_HEADS = 2
GQA_GROUP = N_Q_HEADS // N_KV_HEADS
ATTN_W = N_Q_HEADS * HEAD_DIM
KV_W = N_KV_HEADS * HEAD_DIM
GRID_W = 64
ROPE_THETA = 10000.0
ROPE_FREQS = HEAD_DIM // 4
POOL_WINDOWS = (2, 4, 8, 16)
N_POOL = len(POOL_WINDOWS)
POOL_DG = 128
POOL_W = N_POOL * POOL_DG
NA_HEADS = 4
NA_W = NA_HEADS * HEAD_DIM
NA_ROWS = 8
NA_COLS = 16
TOP_K = 4
MOE_BLOCK = 256
SWIGLU_ALPHA = 1.702
SWIGLU_LIMIT = 7.0
EPS = 1e-6

LANES = 128
POOL_HALO = 16
NEG = -1e30
VMEM_LIMIT = 56 * 1024 * 1024

F32 = jnp.float32
BF16 = jnp.bfloat16
_dot = functools.partial(jnp.dot, preferred_element_type=jnp.float32)


def _dot_nt(a, b):
    return lax.dot_general(a, b, (((1,), (1,)), ((), ())), preferred_element_type=jnp.float32)


def _cp(*sem, **kw):
    return pltpu.CompilerParams(dimension_semantics=sem, vmem_limit_bytes=VMEM_LIMIT, **kw)


def _split_bf16(a):
    hi = a.astype(BF16)
    lo = (a - hi.astype(F32)).astype(BF16)
    return hi, lo


def _resident(shape, index_map):
    return pl.BlockSpec(shape, index_map, pipeline_mode=pl.Buffered(1))


def _ada_kernel(c_ref, w_ref, b_ref, o_ref):
    c = c_ref[...]
    a = c * jax.nn.sigmoid(c)
    ah, al = _split_bf16(a)
    wh, wl = _split_bf16(w_ref[...])
    o_ref[...] = _dot(ah, wh) + _dot(al, wh) + _dot(ah, wl) + b_ref[...]


def _ada(c_all, w_ada, b_ada):
    L, D, D6 = w_ada.shape
    Bp = c_all.shape[0]
    tn = math.gcd(1024, D6)
    return pl.pallas_call(
        _ada_kernel,
        grid=(L, D6 // tn),
        in_specs=[pl.BlockSpec((Bp, D), lambda l, j: (0, 0)),
                  pl.BlockSpec((None, D, tn), lambda l, j: (l, 0, j)),
                  pl.BlockSpec((None, 1, tn), lambda l, j: (l, 0, j))],
        out_specs=pl.BlockSpec((None, Bp, tn), lambda l, j: (l, 0, j)),
        out_shape=jax.ShapeDtypeStruct((L, Bp, D6), F32),
        compiler_params=_cp("parallel", "parallel"),
        name="ada",
    )(c_all, w_ada, b_ada.reshape(L, 1, D6))


def _rms_mod(x, g, sc, sh):
    r = lax.rsqrt(jnp.mean(x * x, axis=-1, keepdims=True) + EPS)
    return (x * r * g) * (1.0 + sc) + sh


def _norm_mod_kernel(x_ref, g_ref, sc_ref, sh_ref, o_ref):
    o_ref[...] = _rms_mod(x_ref[...], g_ref[...], sc_ref[...], sh_ref[...]).astype(o_ref.dtype)


def _mod_spec(D, b0, which):
    return pl.BlockSpec((None, None, 1, D), lambda b, s: (b + b0, which, 0, 0))


def _norm_mod(x, g, mod, b0, i_sc, i_sh):
    B, S, D = x.shape
    ts = min(512, S)
    return pl.pallas_call(
        _norm_mod_kernel,
        grid=(B, S // ts),
        in_specs=[pl.BlockSpec((None, ts, D), lambda b, s: (b, s, 0)),
                  pl.BlockSpec((1, D), lambda b, s: (0, 0)),
                  _mod_spec(D, b0, i_sc), _mod_spec(D, b0, i_sh)],
        out_specs=pl.BlockSpec((None, ts, D), lambda b, s: (b, s, 0)),
        out_shape=jax.ShapeDtypeStruct((B, S, D), BF16),
        compiler_params=_cp("parallel", "parallel"),
        name="norm_mod",
    )(x, g.reshape(1, D), mod, mod)


def _final_norm_kernel(x_ref, g_ref, o_ref):
    x = x_ref[...]
    r = lax.rsqrt(jnp.mean(x * x, axis=-1, keepdims=True) + EPS)
    o_ref[...] = x * r * g_ref[...]


def _final_norm(x, g):
    B, S, D = x.shape
    ts = min(512, S)
    return pl.pallas_call(
        _final_norm_kernel,
        grid=(B, S // ts),
        in_specs=[pl.BlockSpec((None, ts, D), lambda b, s: (b, s, 0)),
                  pl.BlockSpec((1, D), lambda b, s: (0, 0))],
        out_specs=pl.BlockSpec((None, ts, D), lambda b, s: (b, s, 0)),
        out_shape=jax.ShapeDtypeStruct((B, S, D), F32),
        compiler_params=_cp("parallel", "parallel"),
        name="final_norm",
    )(x, g.reshape(1, D))


def _rope_tables(S):
    t = jnp.arange(S)
    pos = jnp.stack([t // GRID_W, t % GRID_W], axis=-1).astype(F32)
    inv = ROPE_THETA ** (-jnp.arange(ROPE_FREQS, dtype=F32) / ROPE_FREQS)
    ang = pos[..., None] * inv
    cos, sin = jnp.cos(ang), jnp.sin(ang)
    zero = jnp.zeros_like(sin)
    c = jnp.stack([cos, cos], axis=2).reshape(S, HEAD_DIM)
    sa = jnp.stack([-sin, zero], axis=2).reshape(S, HEAD_DIM)
    sb = jnp.stack([zero, sin], axis=2).reshape(S, HEAD_DIM)
    return c, sa, sb


def _proj_qk_kernel(a_ref, w_ref, gq_ref, gk_ref, c_ref, sa_ref, sb_ref, q_ref, k_ref):
    acc = _dot(a_ref[...], w_ref[...])
    c, sa, sb = c_ref[...], sa_ref[...], sb_ref[...]
    scale = math.log2(math.e) / math.sqrt(HEAD_DIM)
    for hh in range(N_Q_HEADS + N_KV_HEADS):
        xh = acc[:, hh * HEAD_DIM:(hh + 1) * HEAD_DIM]
        r = lax.rsqrt(jnp.mean(xh * xh, axis=-1, keepdims=True) + EPS)
        is_q = hh < N_Q_HEADS
        xn = xh * r * (gq_ref[...] if is_q else gk_ref[...])
        y = xn * c + pltpu.roll(xn, HEAD_DIM - ROPE_FREQS, 1) * sa + pltpu.roll(xn, ROPE_FREQS, 1) * sb
        if is_q:
            q_ref[:, hh * HEAD_DIM:(hh + 1) * HEAD_DIM] = (y * scale).astype(q_ref.dtype)
        else:
            kk = hh - N_Q_HEADS
            k_ref[:, kk * HEAD_DIM:(kk + 1) * HEAD_DIM] = y.astype(k_ref.dtype)


def _proj_qk(h, w_qk, gq, gk, tabs):
    B, S, D = h.shape
    tm = min(1024, S)
    W = ATTN_W + KV_W
    tab_spec = pl.BlockSpec((tm, HEAD_DIM), lambda b, s: (s, 0))
    vec_spec = pl.BlockSpec((1, HEAD_DIM), lambda b, s: (0, 0))
    return pl.pallas_call(
        _proj_qk_kernel,
        grid=(B, S // tm),
        in_specs=[pl.BlockSpec((None, tm, D), lambda b, s: (b, s, 0)),
                  _resident((D, W), lambda b, s: (0, 0)),
                  vec_spec, vec_spec, tab_spec, tab_spec, tab_spec],
        out_specs=[pl.BlockSpec((None, tm, ATTN_W), lambda b, s: (b, s, 0)),
                   pl.BlockSpec((None, tm, KV_W), lambda b, s: (b, s, 0))],
        out_shape=[jax.ShapeDtypeStruct((B, S, ATTN_W), BF16),
                   jax.ShapeDtypeStruct((B, S, KV_W), BF16)],
        compiler_params=_cp("parallel", "parallel"),
        name="proj_qk",
    )(h, w_qk, gq.reshape(1, HEAD_DIM), gk.reshape(1, HEAD_DIM), *tabs)


MID_SECTIONS = (POOL_W, NA_W, NA_W, NA_W)
ATTN_TK = 512


def _proj_mid_kernel(a_ref, w_ref, wvt_ref, vt_ref, *o_refs):
    a = a_ref[...]
    acc = _dot(a, w_ref[...])
    off = 0
    for o_ref, width in zip(o_refs, MID_SECTIONS):
        o_ref[...] = acc[:, off:off + width].astype(o_ref.dtype)
        off += width
    tk = vt_ref.shape[-1]
    for c in range(vt_ref.shape[0]):
        vt_ref[c] = _dot_nt(wvt_ref[...], a[c * tk:(c + 1) * tk]).astype(vt_ref.dtype)


def _proj_mid(h, w_mid, w_vt):
    B, S, D = h.shape
    tm = min(1024, S)
    tk = min(ATTN_TK, S)
    W = sum(MID_SECTIONS)
    return pl.pallas_call(
        _proj_mid_kernel,
        grid=(B, S // tm),
        in_specs=[pl.BlockSpec((None, tm, D), lambda b, s: (b, s, 0)),
                  _resident((D, W), lambda b, s: (0, 0)),
                  _resident((KV_W, D), lambda b, s: (0, 0))],
        out_specs=[pl.BlockSpec((None, tm // tk, KV_W, tk), lambda b, s: (b, s, 0, 0))]
        + [pl.BlockSpec((None, tm, w), lambda b, s: (b, s, 0)) for w in MID_SECTIONS],
        out_shape=[jax.ShapeDtypeStruct((B, S // tk, KV_W, tk), BF16)]
        + [jax.ShapeDtypeStruct((B, S, w), BF16) for w in MID_SECTIONS],
        compiler_params=_cp("parallel", "parallel"),
        name="proj_mid",
    )(h, w_mid, w_vt)


def _proj_gate_kernel(a_ref, w_ref, o_ref):
    o_ref[...] = jax.nn.sigmoid(_dot(a_ref[...], w_ref[...])).astype(o_ref.dtype)


def _proj_gate(h, w_gate):
    N, D = h.shape
    W = w_gate.shape[1]
    tm = min(1024, N)
    tn = min(2048, D)
    return pl.pallas_call(
        _proj_gate_kernel,
        grid=(W // tn, N // tm),
        in_specs=[pl.BlockSpec((tm, D), lambda j, i: (i, 0)),
                  pl.BlockSpec((D, tn), lambda j, i: (0, j))],
        out_specs=pl.BlockSpec((tm, tn), lambda j, i: (i, j)),
        out_shape=jax.ShapeDtypeStruct((N, W), BF16),
        compiler_params=_cp("parallel", "parallel"),
        name="proj_gate",
    )(h, w_gate)


def _attn_kernel(q_ref, k_ref, vt_ref, o_ref, s_ref):
    n_chunks, _, tk = vt_ref.shape
    tq = q_ref.shape[0]
    heads = range(GQA_GROUP)

    def scores(j, slot):
        k = k_ref[pl.ds(pl.multiple_of(j * tk, tk), tk), :]
        for h in heads:
            s_ref[slot, h] = _dot_nt(k, q_ref[:, h * HEAD_DIM:(h + 1) * HEAD_DIM])

    def consume(j, slot, carry):
        vt = vt_ref[j]
        out = []
        for h in heads:
            m, l, acc = carry[h]
            s = s_ref[slot, h]
            m_new = jnp.maximum(m, jnp.max(s, axis=0, keepdims=True))
            alpha = jnp.exp2(m - m_new)
            p = jnp.exp2(s - m_new)
            l = alpha * l + jnp.sum(p, axis=0, keepdims=True)
            acc = alpha * acc + _dot(vt, p.astype(BF16))
            out.append((m_new, l, acc))
        return tuple(out)

    init = tuple((jnp.full((1, tq), NEG, F32), jnp.zeros((1, tq), F32), jnp.zeros((HEAD_DIM, tq), F32))
                 for _ in heads)
    scores(0, 0)
    if n_chunks % 2:
        def body(j, carry):
            carry = consume(j, 0, carry)
            scores(jnp.minimum(j + 1, n_chunks - 1), 0)
            return carry
        final = lax.fori_loop(0, n_chunks, body, init)
    else:
        def body(jj, carry):
            j = 2 * jj
            scores(j + 1, 1)
            carry = consume(j, 0, carry)
            scores(jnp.minimum(j + 2, n_chunks - 1), 0)
            return consume(j + 1, 1, carry)
        final = lax.fori_loop(0, n_chunks // 2, body, init)
    for h in heads:
        m, l, acc = final[h]
        o_ref[:, h * HEAD_DIM:(h + 1) * HEAD_DIM] = (acc / l).T.astype(o_ref.dtype)


def _attention(q, k, vt):
    B, S, _ = q.shape
    n_chunks, _, tk = vt.shape[1:]
    tq = min(512, S)
    gw = GQA_GROUP * HEAD_DIM
    return pl.pallas_call(
        _attn_kernel,
        grid=(B, N_KV_HEADS, S // tq),
        in_specs=[pl.BlockSpec((None, tq, gw), lambda b, g, i: (b, i, g)),
                  pl.BlockSpec((None, S, HEAD_DIM), lambda b, g, i: (b, 0, g)),
                  pl.BlockSpec((None, n_chunks, HEAD_DIM, tk), lambda b, g, i: (b, 0, g, 0))],
        out_specs=pl.BlockSpec((None, tq, gw), lambda b, g, i: (b, i, g)),
        out_shape=jax.ShapeDtypeStruct((B, S, ATTN_W), BF16),
        scratch_shapes=[pltpu.VMEM((2, GQA_GROUP, tk, tq), F32)],
        compiler_params=_cp("parallel", "parallel", "parallel"),
        name="gqa",
    )(q, k, vt)


def _pool_band():
    T = MOE_BLOCK
    i = np.arange(T)[:, None] + POOL_HALO
    j = np.arange(T + 2 * POOL_HALO)[None, :]
    return np.stack([((j >= i - w // 2) & (j < i + w // 2)) for w in POOL_WINDOWS]).astype(np.float32)


def _pool_kernel(u_ref, band_ref, wp_ref, ps_ref, o_ref, *, T):
    S = u_ref.shape[0]
    nc = S // T
    halo = jnp.zeros((POOL_HALO, POOL_DG), BF16)
    for g in range(N_POOL):
        hw = POOL_WINDOWS[g] // 2
        cols = slice(g * POOL_DG, (g + 1) * POOL_DG)
        for c in range(nc):
            t0 = c * T
            mid = u_ref[t0:t0 + T, cols]
            prev = u_ref[t0 - POOL_HALO:t0, cols] if c > 0 else halo
            nxt = u_ref[t0 + T:t0 + T + POOL_HALO, cols] if c < nc - 1 else halo
            win = _dot(band_ref[g], jnp.concatenate([prev, mid, nxt], axis=0))
            t = t0 + lax.broadcasted_iota(jnp.int32, (T, POOL_DG), 0)
            cnt = (jnp.minimum(t + hw, S) - jnp.maximum(t - hw, 0)).astype(F32)
            pooled = win / cnt - mid.astype(F32)
            mixed = _dot(pooled.astype(BF16), wp_ref[g]) * ps_ref[:, cols]
            o_ref[t0:t0 + T, cols] = mixed.astype(o_ref.dtype)


def _pool(u, band, w_pool, pool_scale):
    B, S, _ = u.shape
    T = band.shape[1]
    return pl.pallas_call(
        functools.partial(_pool_kernel, T=T),
        grid=(B,),
        in_specs=[pl.BlockSpec((None, S, POOL_W), lambda b: (b, 0, 0)),
                  pl.BlockSpec(band.shape, lambda b: (0, 0, 0)),
                  pl.BlockSpec(w_pool.shape, lambda b: (0, 0, 0)),
                  pl.BlockSpec((1, POOL_W), lambda b: (0, 0))],
        out_specs=pl.BlockSpec((None, S, POOL_W), lambda b: (b, 0, 0)),
        out_shape=jax.ShapeDtypeStruct((B, S, POOL_W), BF16),
        compiler_params=_cp("parallel"),
        name="pool",
    )(u, band, w_pool, pool_scale.reshape(1, POOL_W))


def _na_bias(rpb):
    H = rpb.shape[0]
    nr, nc, P = 2 * NA_ROWS - 1, 2 * NA_COLS - 1, 2 * GRID_W
    w = jnp.full((H, nr, P), NEG, F32).at[:, :, GRID_W - NA_COLS:GRID_W - 1 + NA_COLS].set(rpb.reshape(H, nr, nc))
    skew = jnp.tile(w, (1, 1, GRID_W))[:, :, :GRID_W * (P - 1)].reshape(H, nr, GRID_W, P - 1)
    toe = skew[..., GRID_W - 1:2 * GRID_W - 1]
    qc = np.arange(GRID_W)[:, None]
    kc = np.arange(GRID_W)[None, :]
    cs = np.clip(qc - NA_COLS // 2, 0, GRID_W - NA_COLS)
    toe = jnp.where((kc >= cs) & (kc < cs + NA_COLS), toe, NEG)
    per_off = jnp.stack([toe[:, NA_ROWS - 1 - o:2 * NA_ROWS - 1 - o] for o in range(NA_ROWS)], axis=1)
    return per_off.transpose(0, 1, 3, 2, 4).reshape(H, NA_ROWS, GRID_W, NA_ROWS * GRID_W)


def _na_kernel(q_ref, k_ref, v_ref, b_ref, o_ref, *, rb, R):
    i = pl.program_id(1)
    scale = 1.0 / math.sqrt(HEAD_DIM)
    win = NA_ROWS * GRID_W
    for rr in range(rb):
        r = i * rb + rr
        rs = jnp.clip(r - NA_ROWS // 2, 0, R - NA_ROWS)
        off = r - rs
        start = pl.multiple_of(rs * GRID_W, GRID_W)
        rows = slice(rr * GRID_W, (rr + 1) * GRID_W)
        for h in range(NA_HEADS):
            cols = slice(h * HEAD_DIM, (h + 1) * HEAD_DIM)
            s = _dot_nt(q_ref[rows, cols], k_ref[pl.ds(start, win), cols]) * scale + b_ref[h, off]
            m = jnp.max(s, axis=-1, keepdims=True)
            p = jnp.exp(s - m)
            l = jnp.sum(p, axis=-1, keepdims=True)
            out = _dot(p.astype(BF16), v_ref[pl.ds(start, win), cols]) / l
            o_ref[rows, cols] = out.astype(o_ref.dtype)


def _na(q, k, v, bias):
    B, S, _ = q.shape
    R = S // GRID_W
    assert R >= NA_ROWS
    rb = 8
    return pl.pallas_call(
        functools.partial(_na_kernel, rb=rb, R=R),
        grid=(B, R // rb),
        in_specs=[pl.BlockSpec((None, rb * GRID_W, NA_W), lambda b, i: (b, i, 0)),
                  pl.BlockSpec((None, S, NA_W), lambda b, i: (b, 0, 0)),
                  pl.BlockSpec((None, S, NA_W), lambda b, i: (b, 0, 0)),
                  pl.BlockSpec(bias.shape, lambda b, i: (0, 0, 0, 0))],
        out_specs=pl.BlockSpec((None, rb * GRID_W, NA_W), lambda b, i: (b, i, 0)),
        out_shape=jax.ShapeDtypeStruct((B, S, NA_W), BF16),
        compiler_params=_cp("parallel", "parallel"),
        name="natten",
    )(q, k, v, bias)


def _merge_kernel(oa_ref, op_ref, on_ref, ga_ref, gp_ref, gn_ref, x_ref, g1_ref,
                  wa_ref, wp_ref, wn_ref, wo_ref, o_ref):
    m = ga_ref[...].astype(F32) * _dot(oa_ref[...], wa_ref[...])
    m = m + gp_ref[...].astype(F32) * _dot(op_ref[...], wp_ref[...])
    m = m + gn_ref[...].astype(F32) * _dot(on_ref[...], wn_ref[...])
    y = _dot(m.astype(BF16), wo_ref[...])
    o_ref[...] = x_ref[...] + g1_ref[...] * y


def _merge(o_attn, o_pool, o_na, gates, x, mod, b0, w_a, w_p, w_n, w_o):
    B, S, D = x.shape
    tm = min(512, S)
    row = lambda w: pl.BlockSpec((None, tm, w), lambda b, s: (b, s, 0))
    gate = lambda j: pl.BlockSpec((None, tm, D), lambda b, s: (b, s, j))
    res = lambda w: _resident(w.shape, lambda b, s: (0, 0))
    return pl.pallas_call(
        _merge_kernel,
        grid=(B, S // tm),
        in_specs=[row(ATTN_W), row(POOL_W), row(NA_W), gate(0), gate(1), gate(2), row(D),
                  _mod_spec(D, b0, 2), res(w_a), res(w_p), res(w_n), res(w_o)],
        out_specs=row(D),
        out_shape=jax.ShapeDtypeStruct((B, S, D), F32),
        compiler_params=_cp("parallel", "parallel"),
        name="merge",
    )(o_attn, o_pool, o_na, gates, gates, gates, x, mod, w_a, w_p, w_n, w_o)


def _ffn_prep_kernel(x_ref, g_ref, sc_ref, sh_ref, wrh_ref, wrl_ref, br_ref,
                     h_ref, te_ref, tw_ref, rk_ref, cnt_ref, carry_ref):
    first = jnp.logical_and(pl.program_id(0) == 0, pl.program_id(1) == 0)

    @pl.when(first)
    def _():
        carry_ref[...] = jnp.zeros_like(carry_ref)

    tm = x_ref.shape[0]
    h = _rms_mod(x_ref[...], g_ref[...], sc_ref[...], sh_ref[...])
    h_ref[...] = h
    hh, hl = _split_bf16(h)
    wrh = wrh_ref[...]
    logits = _dot(hh, wrh) + _dot(hl, wrh) + _dot(hh, wrl_ref[...]) + br_ref[...]
    lane = lax.broadcasted_iota(jnp.int32, (tm, LANES), 1).astype(F32)
    vals, idxs, hots = [], [], []
    cur = logits
    for _ in range(TOP_K):
        m = jnp.max(cur, axis=-1, keepdims=True)
        idx = jnp.min(jnp.where(cur == m, lane, float(LANES)), axis=-1, keepdims=True)
        hot = lane == idx
        vals.append(m)
        idxs.append(idx)
        hots.append(hot)
        cur = jnp.where(hot, -3e38, cur)
    ex = [jnp.exp(v - vals[0]) for v in vals]
    den = ex[0] + ex[1] + ex[2] + ex[3]
    onehot = sum(hh_.astype(F32) for hh_ in hots)
    ri = lax.broadcasted_iota(jnp.int32, (tm, tm), 0)
    ci = lax.broadcasted_iota(jnp.int32, (tm, tm), 1)
    tri = (ri > ci).astype(BF16)
    carry = carry_ref[0:1, :]
    before = _dot(tri, onehot.astype(BF16)) + carry
    te = jnp.zeros((tm, LANES), F32)
    tw = jnp.zeros((tm, LANES), F32)
    rk = jnp.zeros((tm, LANES), F32)
    for k in range(TOP_K):
        sel = lane == float(k)
        rank = jnp.sum(jnp.where(hots[k], before, 0.0), axis=-1, keepdims=True)
        te = jnp.where(sel, idxs[k], te)
        tw = jnp.where(sel, ex[k] / den, tw)
        rk = jnp.where(sel, rank, rk)
    te_ref[...] = te.astype(jnp.int32)
    tw_ref[...] = tw
    rk_ref[...] = rk.astype(jnp.int32)
    total = carry + jnp.sum(onehot, axis=0, keepdims=True)
    carry_ref[...] = jnp.broadcast_to(total, carry_ref.shape)
    cnt_ref[...] = jnp.broadcast_to(total, cnt_ref.shape).astype(jnp.int32)


def _ffn_prep(x, g, mod, b0, wr_hi, wr_lo, br):
    B, S, D = x.shape
    tm = min(512, S)
    row = lambda w: pl.BlockSpec((None, tm, w), lambda b, s: (b, s, 0))
    const = lambda shape: pl.BlockSpec(shape, lambda b, s: (0, 0))
    return pl.pallas_call(
        _ffn_prep_kernel,
        grid=(B, S // tm),
        in_specs=[row(D), const((1, D)), _mod_spec(D, b0, 4), _mod_spec(D, b0, 3),
                  const((D, LANES)), const((D, LANES)), const((1, LANES))],
        out_specs=[row(D), row(LANES), row(LANES), row(LANES), const((8, LANES))],
        out_shape=[jax.ShapeDtypeStruct((B, S, D), F32),
                   jax.ShapeDtypeStruct((B, S, LANES), jnp.int32),
                   jax.ShapeDtypeStruct((B, S, LANES), F32),
                   jax.ShapeDtypeStruct((B, S, LANES), jnp.int32),
                   jax.ShapeDtypeStruct((8, LANES), jnp.int32)],
        scratch_shapes=[pltpu.VMEM((8, LANES), F32)],
        compiler_params=_cp("arbitrary", "arbitrary"),
        name="ffn_prep",
    )(x, g.reshape(1, D), mod, mod, wr_hi, wr_lo, br)


def _dispatch_kernel(nv_ref, dest_ref, h_ref, xs_ref, zeros, sem, zsem):
    tm = h_ref.shape[0]
    nb = nv_ref.shape[0]

    @pl.when(pl.program_id(0) == 0)
    def _():
        zeros[...] = jnp.zeros_like(zeros)

        def tail_copies(b, fn):
            nv = nv_ref[b]
            base = b * MOE_BLOCK
            up = (nv + 7) // 8 * 8
            for r in range(7):
                @pl.when(nv + r < up)
                def _(r=r):
                    fn(pltpu.make_async_copy(zeros.at[pl.ds(0, 1), :], xs_ref.at[pl.ds(base + nv + r, 1), :], zsem))
            units = (MOE_BLOCK - up) // 8
            off = up
            for bit in (32, 16, 8, 4, 2, 1):
                size = bit * 8
                hit = (units & bit) != 0

                @pl.when(hit)
                def _(size=size, off=off):
                    dst = xs_ref.at[pl.ds(pl.multiple_of(base + off, 8), size), :]
                    fn(pltpu.make_async_copy(zeros.at[pl.ds(0, size), :], dst, zsem))
                off = off + jnp.where(hit, size, 0)

        def start_all(b, c):
            tail_copies(b, lambda cp: cp.start())
            return c

        def wait_all(b, c):
            tail_copies(b, lambda cp: cp.wait())
            return c

        lax.fori_loop(0, nb, start_all, 0)
        lax.fori_loop(0, nb, wait_all, 0)

    def row_copy(r, d):
        return pltpu.make_async_copy(h_ref.at[pl.ds(r, 1), :], xs_ref.at[pl.ds(d, 1), :], sem)

    def issue(r, c):
        for k in range(TOP_K):
            row_copy(r, dest_ref[r * TOP_K + k]).start()
        return c

    lax.fori_loop(0, tm, issue, 0)

    def drain(r, c):
        row_copy(0, 0).wait()
        return c

    lax.fori_loop(0, tm * TOP_K, drain, 0, unroll=8)


def _dispatch(h, dest, n_valid):
    N, D = h.shape
    tm = min(512, N)
    n_slots = n_valid.shape[0] * MOE_BLOCK
    return pl.pallas_call(
        _dispatch_kernel,
        grid_spec=pltpu.PrefetchScalarGridSpec(
            num_scalar_prefetch=1, grid=(N // tm,),
            in_specs=[pl.BlockSpec((tm * TOP_K,), lambda i, nv: (i,), memory_space=pltpu.SMEM),
                      pl.BlockSpec((tm, D), lambda i, nv: (i, 0))],
            out_specs=pl.BlockSpec(memory_space=pl.ANY),
            scratch_shapes=[pltpu.VMEM((MOE_BLOCK, D), F32), pltpu.SemaphoreType.DMA(()),
                            pltpu.SemaphoreType.DMA(())]),
        out_shape=jax.ShapeDtypeStruct((n_slots, D), F32),
        compiler_params=_cp("arbitrary", has_side_effects=True),
        name="dispatch",
    )(n_valid, dest, h)


SPLIT_CHUNK = 2 * LANES


def _split_gu_kernel(w_ref, sel_ref, g_ref, u_ref):
    sel = sel_ref[...]
    for c in range(w_ref.shape[1] // SPLIT_CHUNK):
        chunk = w_ref[:, c * SPLIT_CHUNK:(c + 1) * SPLIT_CHUNK].astype(BF16)
        r = _dot(chunk, sel)
        g_ref[:, c * LANES:(c + 1) * LANES] = r[:, :LANES].astype(g_ref.dtype)
        u_ref[:, c * LANES:(c + 1) * LANES] = r[:, LANES:].astype(u_ref.dtype)


def _split_gu(w_gu):
    L, E, D, F2 = w_gu.shape
    Fd = F2 // 2
    tk = min(256, D)
    src = np.arange(SPLIT_CHUNK)[:, None]
    dst = np.arange(SPLIT_CHUNK)[None, :]
    sel = jnp.asarray(src == 2 * (dst % LANES) + dst // LANES, BF16)
    out_spec = pl.BlockSpec((None, None, tk, Fd), lambda l, e, k: (l, e, k, 0))
    return pl.pallas_call(
        _split_gu_kernel,
        grid=(L, E, D // tk),
        in_specs=[pl.BlockSpec((None, None, tk, F2), lambda l, e, k: (l, e, k, 0)),
                  pl.BlockSpec((SPLIT_CHUNK, SPLIT_CHUNK), lambda l, e, k: (0, 0))],
        out_specs=[out_spec, out_spec],
        out_shape=[jax.ShapeDtypeStruct((L, E, D, Fd), BF16)] * 2,
        compiler_params=_cp("parallel", "parallel", "parallel"),
        name="split_gate_up",
    )(w_gu, sel)


def _gm1_kernel(be_ref, nv_ref, nu_ref, x_ref, wg_ref, wu_ref, bg_ref, bu_ref, o_ref):
    i = pl.program_id(0)

    @pl.when(i < nu_ref[0])
    def _():
        rows = lax.broadcasted_iota(jnp.int32, x_ref.shape, 0)
        x = jnp.where(rows < nv_ref[i], x_ref[...], 0.0).astype(BF16)
        g = _dot(x, wg_ref[...]) + bg_ref[...]
        u = _dot(x, wu_ref[...]) + bu_ref[...]
        gate = jnp.minimum(g, SWIGLU_LIMIT)
        up = jnp.clip(u, -SWIGLU_LIMIT, SWIGLU_LIMIT)
        o_ref[...] = ((up + 1.0) * gate * jax.nn.sigmoid(SWIGLU_ALPHA * gate)).astype(o_ref.dtype)

    @pl.when(i >= nu_ref[0])
    def _():
        o_ref[...] = jnp.zeros_like(o_ref)


def _gm1(xs, block_e, n_valid, n_used, layer, w_g, w_u, b_g, b_u):
    n_slots, D = xs.shape
    Fd = w_g.shape[-1]
    nb = n_slots // MOE_BLOCK
    wspec = pl.BlockSpec((None, None, D, Fd), lambda i, be, nv, nu: (layer, be[i], 0, 0))
    bspec = pl.BlockSpec((None, 1, Fd), lambda i, be, nv, nu: (be[i], 0, 0))
    return pl.pallas_call(
        _gm1_kernel,
        grid_spec=pltpu.PrefetchScalarGridSpec(
            num_scalar_prefetch=3, grid=(nb,),
            in_specs=[pl.BlockSpec((MOE_BLOCK, D), lambda i, be, nv, nu: (i, 0)), wspec, wspec, bspec, bspec],
            out_specs=pl.BlockSpec((MOE_BLOCK, Fd), lambda i, be, nv, nu: (i, 0))),
        out_shape=jax.ShapeDtypeStruct((n_slots, Fd), BF16),
        compiler_params=_cp("arbitrary"),
        name="expert_up",
    )(block_e, n_valid, n_used, xs, w_g, w_u, b_g, b_u)


def _gm2_kernel(be_ref, nu_ref, a_ref, wd_ref, bd_ref, o_ref):
    i = pl.program_id(0)

    @pl.when(i < nu_ref[0])
    def _():
        o_ref[...] = _dot(a_ref[...], wd_ref[...]) + bd_ref[...]

    @pl.when(i >= nu_ref[0])
    def _():
        o_ref[...] = jnp.zeros_like(o_ref)


def _gm2(act, block_e, n_used, layer, w_d, b_d):
    n_slots, Fd = act.shape
    D = w_d.shape[-1]
    nb = n_slots // MOE_BLOCK
    return pl.pallas_call(
        _gm2_kernel,
        grid_spec=pltpu.PrefetchScalarGridSpec(
            num_scalar_prefetch=2, grid=(nb,),
            in_specs=[pl.BlockSpec((MOE_BLOCK, Fd), lambda i, be, nu: (i, 0)),
                      pl.BlockSpec((None, None, Fd, D), lambda i, be, nu: (layer, be[i], 0, 0)),
                      pl.BlockSpec((None, 1, D), lambda i, be, nu: (be[i], 0, 0))],
            out_specs=pl.BlockSpec((MOE_BLOCK, D), lambda i, be, nu: (i, 0))),
        out_shape=jax.ShapeDtypeStruct((n_slots, D), F32),
        compiler_params=_cp("arbitrary"),
        name="expert_down",
    )(block_e, n_used, act, w_d, b_d)


def _combine_kernel(dest_ref, x_ref, tw_ref, g2_ref, y_ref, o_ref, buf, sem):
    tm = x_ref.shape[0]

    def row_copy(r, k, d):
        return pltpu.make_async_copy(y_ref.at[pl.ds(d, 1), :], buf.at[k, pl.ds(r, 1), :], sem)

    def issue(r, c):
        for k in range(TOP_K):
            row_copy(r, k, dest_ref[r * TOP_K + k]).start()
        return c

    lax.fori_loop(0, tm, issue, 0)

    def drain(r, c):
        row_copy(0, 0, 0).wait()
        return c

    lax.fori_loop(0, tm * TOP_K, drain, 0, unroll=8)
    tw = tw_ref[...]
    acc = tw[:, 0:1] * buf[0]
    for k in range(1, TOP_K):
        acc = acc + tw[:, k:k + 1] * buf[k]
    o_ref[...] = x_ref[...] + g2_ref[...] * acc


def _combine(x, tw, mod, b0, y_slots, dest):
    B, S, D = x.shape
    tm = min(256, S)
    spb = S // tm
    row = lambda w: pl.BlockSpec((None, tm, w), lambda b, s: (b, s, 0))
    return pl.pallas_call(
        _combine_kernel,
        grid=(B, spb),
        in_specs=[pl.BlockSpec((tm * TOP_K,), lambda b, s: (b * spb + s,), memory_space=pltpu.SMEM),
                  row(D), row(LANES), _mod_spec(D, b0, 5),
                  pl.BlockSpec(memory_space=pl.ANY)],
        out_specs=row(D),
        out_shape=jax.ShapeDtypeStruct((B, S, D), F32),
        scratch_shapes=[pltpu.VMEM((TOP_K, tm, D), F32), pltpu.SemaphoreType.DMA(())],
        compiler_params=_cp("arbitrary", "arbitrary"),
        name="combine",
    )(dest, x, tw, mod, y_slots)


def _slot_tables(te, rk, cnt, n_blocks, E):
    counts = cnt[0, :E]
    padded = (counts + MOE_BLOCK - 1) // MOE_BLOCK * MOE_BLOCK
    pad_end = jnp.cumsum(padded)
    pad_start = pad_end - padded
    experts = jnp.arange(E, dtype=jnp.int32)

    def lookup(table, idx):
        return jnp.sum(jnp.where(idx[..., None] == experts, table, 0), axis=-1)

    dest = (lookup(pad_start, te) + rk).reshape(-1).astype(jnp.int32)
    blk0 = jnp.arange(n_blocks, dtype=jnp.int32) * MOE_BLOCK
    block_e = jnp.minimum(jnp.sum(pad_end[None, :] <= blk0[:, None], axis=1), E - 1).astype(jnp.int32)
    n_valid = jnp.clip(lookup(counts, block_e) - (blk0 - lookup(pad_start, block_e)), 0, MOE_BLOCK).astype(jnp.int32)
    n_used = (pad_end[-1:] // MOE_BLOCK).astype(jnp.int32)
    return dest, block_e, n_valid, n_used


def _layer(x, b0, mod, tabs, band, p):
    B, S, D = x.shape
    N = B * S
    E = p['w_g'].shape[1]
    h = _norm_mod(x, p['norm_mix_g'], mod, b0, 1, 0)
    q, k = _proj_qk(h, p['w_qk'], p['q_norm_g'], p['k_norm_g'], tabs)
    vt, u, qn, kn, vn = _proj_mid(h, p['w_mid'], p['w_vt'])
    gates = _proj_gate(h.reshape(N, D), p['w_gate']).reshape(B, S, 3 * D)
    o_attn = _attention(q, k, vt)
    o_pool = _pool(u, band, p['w_pool'], p['pool_scale'])
    o_na = _na(qn, kn, vn, p['na_bias'])
    x = _merge(o_attn, o_pool, o_na, gates, x, mod, b0, p['w_br_attn'], p['w_br_pool'], p['w_br_na'], p['w_out'])
    h2, te, tw, rk, cnt = _ffn_prep(x, p['norm_ffn_g'], mod, b0, p['wr_hi'], p['wr_lo'], p['br'])
    n_blocks = -(-N * TOP_K // MOE_BLOCK) + E
    dest, block_e, n_valid, n_used = _slot_tables(te[..., :TOP_K], rk[..., :TOP_K], cnt, n_blocks, E)
    xs = _dispatch(h2.reshape(N, D), dest, n_valid)
    act = _gm1(xs, block_e, n_valid, n_used, p['layer'], p['w_g'], p['w_u'], p['b_g'], p['b_u'])
    y_slots = _gm2(act, block_e, n_used, p['layer'], p['w_d'], p['b_d'])
    return _combine(x, tw, mod, b0, y_slots, dest)


def kernel(x_prompt, x_sample, c_prompt, c_sample, w_ada, b_ada, norm_mix_g, norm_ffn_g, w_in, q_norm_g, k_norm_g, w_pool, pool_scale, rpb, w_br_attn, w_br_pool, w_br_na, w_out, w_router, b_router, w_gu, b_gu, w_down, b_down, final_g):
    L, D, _ = w_ada.shape
    E = w_router.shape[-1]
    Fd = w_down.shape[2]
    Bp, Bs = c_prompt.shape[0], c_sample.shape[0]
    n_c = Bp + Bs
    c_all = jnp.zeros((-(-n_c // 8) * 8, D), F32).at[:n_c].set(jnp.concatenate([c_prompt, c_sample], axis=0))
    mod = _ada(c_all, w_ada, b_ada).reshape(L, c_all.shape[0], 6, 1, D)

    band = jnp.asarray(_pool_band(), BF16)
    w_g, w_u = _split_gu(w_gu)
    w_d = w_down.astype(BF16)
    b_gu2 = b_gu.reshape(L, E, Fd, 2)
    qk_w = ATTN_W + KV_W
    mid_w = sum(MID_SECTIONS)
    wr = jnp.zeros((L, D, LANES), F32).at[:, :, :E].set(w_router)
    wr_hi = wr.astype(BF16)
    wr_lo = (wr - wr_hi.astype(F32)).astype(BF16)
    br = jnp.full((L, 1, LANES), NEG, F32).at[:, 0, :E].set(b_router)

    layers = []
    for l in range(L):
        layers.append(dict(
            norm_mix_g=norm_mix_g[l], norm_ffn_g=norm_ffn_g[l],
            q_norm_g=q_norm_g[l], k_norm_g=k_norm_g[l],
            w_qk=w_in[l, :, :qk_w].astype(BF16),
            w_vt=w_in[l, :, qk_w:qk_w + KV_W].T.astype(BF16),
            w_mid=w_in[l, :, qk_w + KV_W:qk_w + KV_W + mid_w].astype(BF16),
            w_gate=w_in[l, :, qk_w + KV_W + mid_w:].astype(BF16),
            w_pool=w_pool[l].astype(BF16), pool_scale=pool_scale[l],
            na_bias=_na_bias(rpb[l]),
            w_br_attn=w_br_attn[l].astype(BF16), w_br_pool=w_br_pool[l].astype(BF16),
            w_br_na=w_br_na[l].astype(BF16), w_out=w_out[l].astype(BF16),
            wr_hi=wr_hi[l], wr_lo=wr_lo[l], br=br[l],
            layer=l, w_g=w_g, w_u=w_u,
            b_g=b_gu2[l, :, :, 0].reshape(E, 1, Fd), b_u=b_gu2[l, :, :, 1].reshape(E, 1, Fd),
            w_d=w_d, b_d=b_down[l].reshape(E, 1, D),
        ))

    outs = []
    for x, b0 in ((x_prompt, 0), (x_sample, Bp)):
        tabs = _rope_tables(x.shape[1])
        for l in range(L):
            x = _layer(x, b0, mod[l], tabs, band, layers[l])
        outs.append(_final_norm(x, final_g))
    return tuple(outs)
```

```python
import functools
import math

import numpy as np
import jax
import jax.numpy as jnp
from jax import lax
from jax.experimental import pallas as pl
from jax.experimental.pallas import tpu as pltpu

HEAD_DIM = 128
N_Q_HEADS = 8
N_KV_HEADS = 2
GQA_GROUP = N_Q_HEADS // N_KV_HEADS
ATTN_W = N_Q_HEADS * HEAD_DIM
KV_W = N_KV_HEADS * HEAD_DIM
GRID_W = 64
ROPE_THETA = 10000.0
ROPE_FREQS = HEAD_DIM // 4
POOL_WINDOWS = (2, 4, 8, 16)
N_POOL = len(POOL_WINDOWS)
POOL_DG = 128
POOL_W = N_POOL * POOL_DG
NA_HEADS = 4
NA_W = NA_HEADS * HEAD_DIM
NA_ROWS = 8
NA_COLS = 16
TOP_K = 4
MOE_BLOCK = 256
SWIGLU_ALPHA = 1.702
SWIGLU_LIMIT = 7.0
EPS = 1e-6

LANES = 128
POOL_HALO = 16
NEG = -1e30
VMEM_LIMIT = 56 * 1024 * 1024

F32 = jnp.float32
BF16 = jnp.bfloat16
_dot = functools.partial(jnp.dot, preferred_element_type=jnp.float32)


def _dot_nt(a, b):
    return lax.dot_general(a, b, (((1,), (1,)), ((), ())), preferred_element_type=jnp.float32)


def _cp(*sem, **kw):
    return pltpu.CompilerParams(dimension_semantics=sem, vmem_limit_bytes=VMEM_LIMIT, **kw)


def _split_bf16(a):
    hi = a.astype(BF16)
    lo = (a - hi.astype(F32)).astype(BF16)
    return hi, lo


def _resident(shape, index_map):
    return pl.BlockSpec(shape, index_map, pipeline_mode=pl.Buffered(1))


def _ada_kernel(c_ref, w_ref, b_ref, o_ref):
    c = c_ref[...]
    a = c * jax.nn.sigmoid(c)
    ah, al = _split_bf16(a)
    wh, wl = _split_bf16(w_ref[...])
    o_ref[...] = _dot(ah, wh) + _dot(al, wh) + _dot(ah, wl) + b_ref[...]


def _ada(c_all, w_ada, b_ada):
    L, D, D6 = w_ada.shape
    Bp = c_all.shape[0]
    tn = math.gcd(1024, D6)
    return pl.pallas_call(
        _ada_kernel,
        grid=(L, D6 // tn),
        in_specs=[pl.BlockSpec((Bp, D), lambda l, j: (0, 0)),
                  pl.BlockSpec((None, D, tn), lambda l, j: (l, 0, j)),
                  pl.BlockSpec((None, 1, tn), lambda l, j: (l, 0, j))],
        out_specs=pl.BlockSpec((None, Bp, tn), lambda l, j: (l, 0, j)),
        out_shape=jax.ShapeDtypeStruct((L, Bp, D6), F32),
        compiler_params=_cp("parallel", "parallel"),
        name="ada",
    )(c_all, w_ada, b_ada.reshape(L, 1, D6))


def _rms_mod(x, g, sc, sh):
    r = lax.rsqrt(jnp.mean(x * x, axis=-1, keepdims=True) + EPS)
    return (x * r * g) * (1.0 + sc) + sh


def _norm_mod_kernel(x_ref, g_ref, sc_ref, sh_ref, o_ref):
    o_ref[...] = _rms_mod(x_ref[...], g_ref[...], sc_ref[...], sh_ref[...]).astype(o_ref.dtype)


def _mod_spec(D, b0, which):
    return pl.BlockSpec((None, None, 1, D), lambda b, s: (b + b0, which, 0, 0))


def _norm_mod(x, g, mod, b0, i_sc, i_sh):
    B, S, D = x.shape
    ts = min(512, S)
    return pl.pallas_call(
        _norm_mod_kernel,
        grid=(B, S // ts),
        in_specs=[pl.BlockSpec((None, ts, D), lambda b, s: (b, s, 0)),
                  pl.BlockSpec((1, D), lambda b, s: (0, 0)),
                  _mod_spec(D, b0, i_sc), _mod_spec(D, b0, i_sh)],
        out_specs=pl.BlockSpec((None, ts, D), lambda b, s: (b, s, 0)),
        out_shape=jax.ShapeDtypeStruct((B, S, D), BF16),
        compiler_params=_cp("parallel", "parallel"),
        name="norm_mod",
    )(x, g.reshape(1, D), mod, mod)


def _final_norm_kernel(x_ref, g_ref, o_ref):
    x = x_ref[...]
    r = lax.rsqrt(jnp.mean(x * x, axis=-1, keepdims=True) + EPS)
    o_ref[...] = x * r * g_ref[...]


def _final_norm(x, g):
    B, S, D = x.shape
    ts = min(512, S)
    return pl.pallas_call(
        _final_norm_kernel,
        grid=(B, S // ts),
        in_specs=[pl.BlockSpec((None, ts, D), lambda b, s: (b, s, 0)),
                  pl.BlockSpec((1, D), lambda b, s: (0, 0))],
        out_specs=pl.BlockSpec((None, ts, D), lambda b, s: (b, s, 0)),
        out_shape=jax.ShapeDtypeStruct((B, S, D), F32),
        compiler_params=_cp("parallel", "parallel"),
        name="final_norm",
    )(x, g.reshape(1, D))


def _rope_tables(S):
    t = jnp.arange(S)
    pos = jnp.stack([t // GRID_W, t % GRID_W], axis=-1).astype(F32)
    inv = ROPE_THETA ** (-jnp.arange(ROPE_FREQS, dtype=F32) / ROPE_FREQS)
    ang = pos[..., None] * inv
    cos, sin = jnp.cos(ang), jnp.sin(ang)
    zero = jnp.zeros_like(sin)
    c = jnp.stack([cos, cos], axis=2).reshape(S, HEAD_DIM)
    sa = jnp.stack([-sin, zero], axis=2).reshape(S, HEAD_DIM)
    sb = jnp.stack([zero, sin], axis=2).reshape(S, HEAD_DIM)
    return c, sa, sb


def _proj_qk_kernel(a_ref, w_ref, gq_ref, gk_ref, c_ref, sa_ref, sb_ref, q_ref, k_ref):
    acc = _dot(a_ref[...], w_ref[...])
    c, sa, sb = c_ref[...], sa_ref[...], sb_ref[...]
    scale = math.log2(math.e) / math.sqrt(HEAD_DIM)
    heads = range(N_Q_HEADS + N_KV_HEADS)
    xs = [acc[:, hh * HEAD_DIM:(hh + 1) * HEAD_DIM] for hh in heads]
    rs = [lax.rsqrt(jnp.mean(x * x, axis=-1, keepdims=True) + EPS) for x in xs]
    xn = [xs[hh] * rs[hh] * (gq_ref[...] if hh < N_Q_HEADS else gk_ref[...]) for hh in heads]
    up = [pltpu.roll(x, HEAD_DIM - ROPE_FREQS, 1) for x in xn]
    dn = [pltpu.roll(x, ROPE_FREQS, 1) for x in xn]
    for hh in heads:
        y = xn[hh] * c + up[hh] * sa + dn[hh] * sb
        if hh < N_Q_HEADS:
            q_ref[:, hh * HEAD_DIM:(hh + 1) * HEAD_DIM] = (y * scale).astype(q_ref.dtype)
        else:
            kk = hh - N_Q_HEADS
            k_ref[:, kk * HEAD_DIM:(kk + 1) * HEAD_DIM] = y.astype(k_ref.dtype)


def _proj_qk(h, w_qk, gq, gk, tabs):
    B, S, D = h.shape
    tm = min(1024, S)
    W = ATTN_W + KV_W
    tab_spec = pl.BlockSpec((tm, HEAD_DIM), lambda b, s: (s, 0))
    vec_spec = pl.BlockSpec((1, HEAD_DIM), lambda b, s: (0, 0))
    return pl.pallas_call(
        _proj_qk_kernel,
        grid=(B, S // tm),
        in_specs=[pl.BlockSpec((None, tm, D), lambda b, s: (b, s, 0)),
                  _resident((D, W), lambda b, s: (0, 0)),
                  vec_spec, vec_spec, tab_spec, tab_spec, tab_spec],
        out_specs=[pl.BlockSpec((None, tm, ATTN_W), lambda b, s: (b, s, 0)),
                   pl.BlockSpec((None, tm, KV_W), lambda b, s: (b, s, 0))],
        out_shape=[jax.ShapeDtypeStruct((B, S, ATTN_W), BF16),
                   jax.ShapeDtypeStruct((B, S, KV_W), BF16)],
        compiler_params=_cp("parallel", "parallel"),
        name="proj_qk",
    )(h, w_qk, gq.reshape(1, HEAD_DIM), gk.reshape(1, HEAD_DIM), *tabs)


MID_SECTIONS = (POOL_W, NA_W, NA_W, NA_W)
ATTN_TK = 512


def _proj_mid_kernel(a_ref, w_ref, wvt_ref, vt_ref, *o_refs):
    a = a_ref[...]
    acc = _dot(a, w_ref[...])
    off = 0
    for o_ref, width in zip(o_refs, MID_SECTIONS):
        o_ref[...] = acc[:, off:off + width].astype(o_ref.dtype)
        off += width
    tk = vt_ref.shape[-1]
    for c in range(vt_ref.shape[0]):
        vt_ref[c] = _dot_nt(wvt_ref[...], a[c * tk:(c + 1) * tk]).astype(vt_ref.dtype)


def _proj_mid(h, w_mid, w_vt):
    B, S, D = h.shape
    tm = min(1024, S)
    tk = min(ATTN_TK, S)
    W = sum(MID_SECTIONS)
    return pl.pallas_call(
        _proj_mid_kernel,
        grid=(B, S // tm),
        in_specs=[pl.BlockSpec((None, tm, D), lambda b, s: (b, s, 0)),
                  _resident((D, W), lambda b, s: (0, 0)),
                  _resident((KV_W, D), lambda b, s: (0, 0))],
        out_specs=[pl.BlockSpec((None, tm // tk, KV_W, tk), lambda b, s: (b, s, 0, 0))]
        + [pl.BlockSpec((None, tm, w), lambda b, s: (b, s, 0)) for w in MID_SECTIONS],
        out_shape=[jax.ShapeDtypeStruct((B, S // tk, KV_W, tk), BF16)]
        + [jax.ShapeDtypeStruct((B, S, w), BF16) for w in MID_SECTIONS],
        compiler_params=_cp("parallel", "parallel"),
        name="proj_mid",
    )(h, w_mid, w_vt)


def _proj_gate_kernel(a_ref, w_ref, o_ref):
    o_ref[...] = jax.nn.sigmoid(_dot(a_ref[...], w_ref[...])).astype(o_ref.dtype)


def _proj_gate(h, w_gate):
    N, D = h.shape
    W = w_gate.shape[1]
    tm = min(1024, N)
    tn = min(2048, D)
    return pl.pallas_call(
        _proj_gate_kernel,
        grid=(W // tn, N // tm),
        in_specs=[pl.BlockSpec((tm, D), lambda j, i: (i, 0)),
                  pl.BlockSpec((D, tn), lambda j, i: (0, j))],
        out_specs=pl.BlockSpec((tm, tn), lambda j, i: (i, j)),
        out_shape=jax.ShapeDtypeStruct((N, W), BF16),
        compiler_params=_cp("parallel", "parallel"),
        name="proj_gate",
    )(h, w_gate)


def _attn_kernel(q_ref, k_ref, vt_ref, o_ref, s_ref):
    n_chunks, _, tk = vt_ref.shape
    tq = q_ref.shape[0]
    heads = range(GQA_GROUP)

    def scores(j, slot):
        k = k_ref[pl.ds(pl.multiple_of(j * tk, tk), tk), :]
        for h in heads:
            s_ref[slot, h] = _dot_nt(k, q_ref[:, h * HEAD_DIM:(h + 1) * HEAD_DIM])

    def consume(j, slot, carry):
        vt = vt_ref[j]
        m_new = [jnp.maximum(carry[h][0], jnp.max(s_ref[slot, h], axis=0, keepdims=True)) for h in heads]
        alpha = [jnp.exp2(carry[h][0] - m_new[h]) for h in heads]
        p = [jnp.exp2(s_ref[slot, h] - m_new[h]) for h in heads]
        l = [alpha[h] * carry[h][1] + jnp.sum(p[h], axis=0, keepdims=True) for h in heads]
        pv = [_dot(vt, p[h].astype(BF16)) for h in heads]
        return tuple((m_new[h], l[h], alpha[h] * carry[h][2] + pv[h]) for h in heads)

    init = tuple((jnp.full((1, tq), NEG, F32), jnp.zeros((1, tq), F32), jnp.zeros((HEAD_DIM, tq), F32))
                 for _ in heads)
    scores(0, 0)
    if n_chunks % 2:
        def body(j, carry):
            carry = consume(j, 0, carry)
            scores(jnp.minimum(j + 1, n_chunks - 1), 0)
            return carry
        final = lax.fori_loop(0, n_chunks, body, init)
    else:
        def body(jj, carry):
            j = 2 * jj
            scores(j + 1, 1)
            carry = consume(j, 0, carry)
            scores(jnp.minimum(j + 2, n_chunks - 1), 0)
            return consume(j + 1, 1, carry)
        final = lax.fori_loop(0, n_chunks // 2, body, init)
    for h in heads:
        m, l, acc = final[h]
        o_ref[:, h * HEAD_DIM:(h + 1) * HEAD_DIM] = (acc / l).T.astype(o_ref.dtype)


def _attention(q, k, vt):
    B, S, _ = q.shape
    n_chunks, _, tk = vt.shape[1:]
    tq = min(512, S)
    gw = GQA_GROUP * HEAD_DIM
    return pl.pallas_call(
        _attn_kernel,
        grid=(B, N_KV_HEADS, S // tq),
        in_specs=[pl.BlockSpec((None, tq, gw), lambda b, g, i: (b, i, g)),
                  pl.BlockSpec((None, S, HEAD_DIM), lambda b, g, i: (b, 0, g)),
                  pl.BlockSpec((None, n_chunks, HEAD_DIM, tk), lambda b, g, i: (b, 0, g, 0))],
        out_specs=pl.BlockSpec((None, tq, gw), lambda b, g, i: (b, i, g)),
        out_shape=jax.ShapeDtypeStruct((B, S, ATTN_W), BF16),
        scratch_shapes=[pltpu.VMEM((2, GQA_GROUP, tk, tq), F32)],
        compiler_params=_cp("parallel", "parallel", "parallel"),
        name="gqa",
    )(q, k, vt)


def _pool_band():
    T = MOE_BLOCK
    i = np.arange(T)[:, None] + POOL_HALO
    j = np.arange(T + 2 * POOL_HALO)[None, :]
    return np.stack([((j >= i - w // 2) & (j < i + w // 2)) for w in POOL_WINDOWS]).astype(np.float32)


def _pool_kernel(u_ref, band_ref, wp_ref, ps_ref, o_ref, *, T):
    S = u_ref.shape[0]
    nc = S // T
    halo = jnp.zeros((POOL_HALO, POOL_DG), BF16)
    for g in range(N_POOL):
        hw = POOL_WINDOWS[g] // 2
        cols = slice(g * POOL_DG, (g + 1) * POOL_DG)
        for c in range(nc):
            t0 = c * T
            mid = u_ref[t0:t0 + T, cols]
            prev = u_ref[t0 - POOL_HALO:t0, cols] if c > 0 else halo
            nxt = u_ref[t0 + T:t0 + T + POOL_HALO, cols] if c < nc - 1 else halo
            win = _dot(band_ref[g], jnp.concatenate([prev, mid, nxt], axis=0))
            t = t0 + lax.broadcasted_iota(jnp.int32, (T, POOL_DG), 0)
            cnt = (jnp.minimum(t + hw, S) - jnp.maximum(t - hw, 0)).astype(F32)
            pooled = win / cnt - mid.astype(F32)
            mixed = _dot(pooled.astype(BF16), wp_ref[g]) * ps_ref[:, cols]
            o_ref[t0:t0 + T, cols] = mixed.astype(o_ref.dtype)


def _pool(u, band, w_pool, pool_scale):
    B, S, _ = u.shape
    T = band.shape[1]
    return pl.pallas_call(
        functools.partial(_pool_kernel, T=T),
        grid=(B,),
        in_specs=[pl.BlockSpec((None, S, POOL_W), lambda b: (b, 0, 0)),
                  pl.BlockSpec(band.shape, lambda b: (0, 0, 0)),
                  pl.BlockSpec(w_pool.shape, lambda b: (0, 0, 0)),
                  pl.BlockSpec((1, POOL_W), lambda b: (0, 0))],
        out_specs=pl.BlockSpec((None, S, POOL_W), lambda b: (b, 0, 0)),
        out_shape=jax.ShapeDtypeStruct((B, S, POOL_W), BF16),
        compiler_params=_cp("parallel"),
        name="pool",
    )(u, band, w_pool, pool_scale.reshape(1, POOL_W))


def _na_bias(rpb):
    H = rpb.shape[0]
    nr, nc, P = 2 * NA_ROWS - 1, 2 * NA_COLS - 1, 2 * GRID_W
    w = jnp.full((H, nr, P), NEG, F32).at[:, :, GRID_W - NA_COLS:GRID_W - 1 + NA_COLS].set(rpb.reshape(H, nr, nc))
    skew = jnp.tile(w, (1, 1, GRID_W))[:, :, :GRID_W * (P - 1)].reshape(H, nr, GRID_W, P - 1)
    toe = skew[..., GRID_W - 1:2 * GRID_W - 1]
    qc = np.arange(GRID_W)[:, None]
    kc = np.arange(GRID_W)[None, :]
    cs = np.clip(qc - NA_COLS // 2, 0, GRID_W - NA_COLS)
    toe = jnp.where((kc >= cs) & (kc < cs + NA_COLS), toe, NEG)
    per_off = jnp.stack([toe[:, NA_ROWS - 1 - o:2 * NA_ROWS - 1 - o] for o in range(NA_ROWS)], axis=1)
    return per_off.transpose(0, 1, 3, 2, 4).reshape(H, NA_ROWS, GRID_W, NA_ROWS * GRID_W)


def _na_kernel(q_ref, k_ref, v_ref, b_ref, o_ref, *, rb, R):
    i = pl.program_id(1)
    scale = 1.0 / math.sqrt(HEAD_DIM)
    win = NA_ROWS * GRID_W
    def window(rr):
        r = i * rb + rr
        rs = jnp.clip(r - NA_ROWS // 2, 0, R - NA_ROWS)
        return r - rs, pl.multiple_of(rs * GRID_W, GRID_W)

    def scores(rr, h):
        off, start = window(rr)
        cols = slice(h * HEAD_DIM, (h + 1) * HEAD_DIM)
        q = q_ref[rr * GRID_W:(rr + 1) * GRID_W, cols]
        return _dot_nt(q, k_ref[pl.ds(start, win), cols]) * scale + b_ref[h, off]

    def finish(rr, h, s):
        _, start = window(rr)
        cols = slice(h * HEAD_DIM, (h + 1) * HEAD_DIM)
        m = jnp.max(s, axis=-1, keepdims=True)
        p = jnp.exp(s - m)
        l = jnp.sum(p, axis=-1, keepdims=True)
        out = _dot(p.astype(BF16), v_ref[pl.ds(start, win), cols]) / l
        o_ref[rr * GRID_W:(rr + 1) * GRID_W, cols] = out.astype(o_ref.dtype)

    todo = [(rr, h) for rr in range(rb) for h in range(NA_HEADS)]
    s_next = scores(*todo[0])
    for n, (rr, h) in enumerate(todo):
        s_cur = s_next
        if n + 1 < len(todo):
            s_next = scores(*todo[n + 1])
        finish(rr, h, s_cur)


def _na(q, k, v, bias):
    B, S, _ = q.shape
    R = S // GRID_W
    assert R >= NA_ROWS
    rb = 8
    return pl.pallas_call(
        functools.partial(_na_kernel, rb=rb, R=R),
        grid=(B, R // rb),
        in_specs=[pl.BlockSpec((None, rb * GRID_W, NA_W), lambda b, i: (b, i, 0)),
                  pl.BlockSpec((None, S, NA_W), lambda b, i: (b, 0, 0)),
                  pl.BlockSpec((None, S, NA_W), lambda b, i: (b, 0, 0)),
                  pl.BlockSpec(bias.shape, lambda b, i: (0, 0, 0, 0))],
        out_specs=pl.BlockSpec((None, rb * GRID_W, NA_W), lambda b, i: (b, i, 0)),
        out_shape=jax.ShapeDtypeStruct((B, S, NA_W), BF16),
        compiler_params=_cp("parallel", "parallel"),
        name="natten",
    )(q, k, v, bias)


def _merge_kernel(oa_ref, op_ref, on_ref, ga_ref, gp_ref, gn_ref, x_ref, g1_ref,
                  wa_ref, wp_ref, wn_ref, wo_ref, o_ref):
    m = ga_ref[...].astype(F32) * _dot(oa_ref[...], wa_ref[...])
    m = m + gp_ref[...].astype(F32) * _dot(op_ref[...], wp_ref[...])
    m = m + gn_ref[...].astype(F32) * _dot(on_ref[...], wn_ref[...])
    y = _dot(m.astype(BF16), wo_ref[...])
    o_ref[...] = x_ref[...] + g1_ref[...] * y


def _merge(o_attn, o_pool, o_na, gates, x, mod, b0, w_a, w_p, w_n, w_o):
    B, S, D = x.shape
    tm = min(512, S)
    row = lambda w: pl.BlockSpec((None, tm, w), lambda b, s: (b, s, 0))
    gate = lambda j: pl.BlockSpec((None, tm, D), lambda b, s: (b, s, j))
    res = lambda w: _resident(w.shape, lambda b, s: (0, 0))
    return pl.pallas_call(
        _merge_kernel,
        grid=(B, S // tm),
        in_specs=[row(ATTN_W), row(POOL_W), row(NA_W), gate(0), gate(1), gate(2), row(D),
                  _mod_spec(D, b0, 2), res(w_a), res(w_p), res(w_n), res(w_o)],
        out_specs=row(D),
        out_shape=jax.ShapeDtypeStruct((B, S, D), F32),
        compiler_params=_cp("parallel", "parallel"),
        name="merge",
    )(o_attn, o_pool, o_na, gates, gates, gates, x, mod, w_a, w_p, w_n, w_o)


def _ffn_prep_kernel(x_ref, g_ref, sc_ref, sh_ref, wrh_ref, wrl_ref, br_ref,
                     h_ref, te_ref, tw_ref, rk_ref, cnt_ref, carry_ref):
    first = jnp.logical_and(pl.program_id(0) == 0, pl.program_id(1) == 0)

    @pl.when(first)
    def _():
        carry_ref[...] = jnp.zeros_like(carry_ref)

    tm = x_ref.shape[0]
    h = _rms_mod(x_ref[...], g_ref[...], sc_ref[...], sh_ref[...])
    h_ref[...] = h
    hh, hl = _split_bf16(h)
    wrh = wrh_ref[...]
    logits = _dot(hh, wrh) + _dot(hl, wrh) + _dot(hh, wrl_ref[...]) + br_ref[...]
    lane = lax.broadcasted_iota(jnp.int32, (tm, LANES), 1).astype(F32)
    vals, idxs, hots = [], [], []
    cur = logits
    for _ in range(TOP_K):
        m = jnp.max(cur, axis=-1, keepdims=True)
        idx = jnp.min(jnp.where(cur == m, lane, float(LANES)), axis=-1, keepdims=True)
        hot = lane == idx
        vals.append(m)
        idxs.append(idx)
        hots.append(hot)
        cur = jnp.where(hot, -3e38, cur)
    ex = [jnp.exp(v - vals[0]) for v in vals]
    den = ex[0] + ex[1] + ex[2] + ex[3]
    onehot = sum(hh_.astype(F32) for hh_ in hots)
    ri = lax.broadcasted_iota(jnp.int32, (tm, tm), 0)
    ci = lax.broadcasted_iota(jnp.int32, (tm, tm), 1)
    tri = (ri > ci).astype(BF16)
    carry = carry_ref[0:1, :]
    before = _dot(tri, onehot.astype(BF16)) + carry
    te = jnp.zeros((tm, LANES), F32)
    tw = jnp.zeros((tm, LANES), F32)
    rk = jnp.zeros((tm, LANES), F32)
    for k in range(TOP_K):
        sel = lane == float(k)
        rank = jnp.sum(jnp.where(hots[k], before, 0.0), axis=-1, keepdims=True)
        te = jnp.where(sel, idxs[k], te)
        tw = jnp.where(sel, ex[k] / den, tw)
        rk = jnp.where(sel, rank, rk)
    te_ref[...] = te.astype(jnp.int32)
    tw_ref[...] = tw
    rk_ref[...] = rk.astype(jnp.int32)
    total = carry + jnp.sum(onehot, axis=0, keepdims=True)
    carry_ref[...] = jnp.broadcast_to(total, carry_ref.shape)
    cnt_ref[...] = jnp.broadcast_to(total, cnt_ref.shape).astype(jnp.int32)


def _ffn_prep(x, g, mod, b0, wr_hi, wr_lo, br):
    B, S, D = x.shape
    tm = min(512, S)
    row = lambda w: pl.BlockSpec((None, tm, w), lambda b, s: (b, s, 0))
    const = lambda shape: pl.BlockSpec(shape, lambda b, s: (0, 0))
    return pl.pallas_call(
        _ffn_prep_kernel,
        grid=(B, S // tm),
        in_specs=[row(D), const((1, D)), _mod_spec(D, b0, 4), _mod_spec(D, b0, 3),
                  const((D, LANES)), const((D, LANES)), const((1, LANES))],
        out_specs=[row(D), row(LANES), row(LANES), row(LANES), const((8, LANES))],
        out_shape=[jax.ShapeDtypeStruct((B, S, D), F32),
                   jax.ShapeDtypeStruct((B, S, LANES), jnp.int32),
                   jax.ShapeDtypeStruct((B, S, LANES), F32),
                   jax.ShapeDtypeStruct((B, S, LANES), jnp.int32),
                   jax.ShapeDtypeStruct((8, LANES), jnp.int32)],
        scratch_shapes=[pltpu.VMEM((8, LANES), F32)],
        compiler_params=_cp("arbitrary", "arbitrary"),
        name="ffn_prep",
    )(x, g.reshape(1, D), mod, mod, wr_hi, wr_lo, br)


def _dispatch_kernel(nv_ref, dest_ref, h_ref, xs_ref, zeros, sem, zsem):
    tm = h_ref.shape[0]
    nb = nv_ref.shape[0]

    @pl.when(pl.program_id(0) == 0)
    def _():
        zeros[...] = jnp.zeros_like(zeros)

        def tail_copies(b, fn):
            nv = nv_ref[b]
            base = b * MOE_BLOCK
            up = (nv + 7) // 8 * 8
            for r in range(7):
                @pl.when(nv + r < up)
                def _(r=r):
                    fn(pltpu.make_async_copy(zeros.at[pl.ds(0, 1), :], xs_ref.at[pl.ds(base + nv + r, 1), :], zsem))
            units = (MOE_BLOCK - up) // 8
            off = up
            for bit in (32, 16, 8, 4, 2, 1):
                size = bit * 8
                hit = (units & bit) != 0

                @pl.when(hit)
                def _(size=size, off=off):
                    dst = xs_ref.at[pl.ds(pl.multiple_of(base + off, 8), size), :]
                    fn(pltpu.make_async_copy(zeros.at[pl.ds(0, size), :], dst, zsem))
                off = off + jnp.where(hit, size, 0)

        def start_all(b, c):
            tail_copies(b, lambda cp: cp.start())
            return c

        def wait_all(b, c):
            tail_copies(b, lambda cp: cp.wait())
            return c

        lax.fori_loop(0, nb, start_all, 0)
        lax.fori_loop(0, nb, wait_all, 0)

    def row_copy(r, d):
        return pltpu.make_async_copy(h_ref.at[pl.ds(r, 1), :], xs_ref.at[pl.ds(d, 1), :], sem)

    def issue(r, c):
        for k in range(TOP_K):
            row_copy(r, dest_ref[r * TOP_K + k]).start()
        return c

    lax.fori_loop(0, tm, issue, 0)

    def drain(r, c):
        row_copy(0, 0).wait()
        return c

    lax.fori_loop(0, tm * TOP_K, drain, 0, unroll=8)


def _dispatch(h, dest, n_valid):
    N, D = h.shape
    tm = min(512, N)
    n_slots = n_valid.shape[0] * MOE_BLOCK
    return pl.pallas_call(
        _dispatch_kernel,
        grid_spec=pltpu.PrefetchScalarGridSpec(
            num_scalar_prefetch=1, grid=(N // tm,),
            in_specs=[pl.BlockSpec((tm * TOP_K,), lambda i, nv: (i,), memory_space=pltpu.SMEM),
                      pl.BlockSpec((tm, D), lambda i, nv: (i, 0))],
            out_specs=pl.BlockSpec(memory_space=pl.ANY),
            scratch_shapes=[pltpu.VMEM((MOE_BLOCK, D), F32), pltpu.SemaphoreType.DMA(()),
                            pltpu.SemaphoreType.DMA(())]),
        out_shape=jax.ShapeDtypeStruct((n_slots, D), F32),
        compiler_params=_cp("arbitrary", has_side_effects=True),
        name="dispatch",
    )(n_valid, dest, h)


SPLIT_CHUNK = 2 * LANES


def _split_gu_kernel(w_ref, sel_ref, g_ref, u_ref):
    sel = sel_ref[...]
    for c in range(w_ref.shape[1] // SPLIT_CHUNK):
        chunk = w_ref[:, c * SPLIT_CHUNK:(c + 1) * SPLIT_CHUNK].astype(BF16)
        r = _dot(chunk, sel)
        g_ref[:, c * LANES:(c + 1) * LANES] = r[:, :LANES].astype(g_ref.dtype)
        u_ref[:, c * LANES:(c + 1) * LANES] = r[:, LANES:].astype(u_ref.dtype)


def _split_gu(w_gu):
    L, E, D, F2 = w_gu.shape
    Fd = F2 // 2
    tk = min(256, D)
    src = np.arange(SPLIT_CHUNK)[:, None]
    dst = np.arange(SPLIT_CHUNK)[None, :]
    sel = jnp.asarray(src == 2 * (dst % LANES) + dst // LANES, BF16)
    out_spec = pl.BlockSpec((None, None, tk, Fd), lambda l, e, k: (l, e, k, 0))
    return pl.pallas_call(
        _split_gu_kernel,
        grid=(L, E, D // tk),
        in_specs=[pl.BlockSpec((None, None, tk, F2), lambda l, e, k: (l, e, k, 0)),
                  pl.BlockSpec((SPLIT_CHUNK, SPLIT_CHUNK), lambda l, e, k: (0, 0))],
        out_specs=[out_spec, out_spec],
        out_shape=[jax.ShapeDtypeStruct((L, E, D, Fd), BF16)] * 2,
        compiler_params=_cp("parallel", "parallel", "parallel"),
        name="split_gate_up",
    )(w_gu, sel)


def _gm1_kernel(be_ref, nv_ref, nu_ref, x_ref, wg_ref, wu_ref, bg_ref, bu_ref, o_ref):
    i = pl.program_id(0)

    @pl.when(i < nu_ref[0])
    def _():
        rows = lax.broadcasted_iota(jnp.int32, x_ref.shape, 0)
        x = jnp.where(rows < nv_ref[i], x_ref[...], 0.0).astype(BF16)
        g = _dot(x, wg_ref[...]) + bg_ref[...]
        u = _dot(x, wu_ref[...]) + bu_ref[...]
        gate = jnp.minimum(g, SWIGLU_LIMIT)
        up = jnp.clip(u, -SWIGLU_LIMIT, SWIGLU_LIMIT)
        o_ref[...] = ((up + 1.0) * gate * jax.nn.sigmoid(SWIGLU_ALPHA * gate)).astype(o_ref.dtype)

    @pl.when(i >= nu_ref[0])
    def _():
        o_ref[...] = jnp.zeros_like(o_ref)


def _gm1(xs, block_e, n_valid, n_used, layer, w_g, w_u, b_g, b_u):
    n_slots, D = xs.shape
    Fd = w_g.shape[-1]
    nb = n_slots // MOE_BLOCK
    wspec = pl.BlockSpec((None, None, D, Fd), lambda i, be, nv, nu: (layer, be[i], 0, 0))
    bspec = pl.BlockSpec((None, 1, Fd), lambda i, be, nv, nu: (be[i], 0, 0))
    return pl.pallas_call(
        _gm1_kernel,
        grid_spec=pltpu.PrefetchScalarGridSpec(
            num_scalar_prefetch=3, grid=(nb,),
            in_specs=[pl.BlockSpec((MOE_BLOCK, D), lambda i, be, nv, nu: (i, 0)), wspec, wspec, bspec, bspec],
            out_specs=pl.BlockSpec((MOE_BLOCK, Fd), lambda i, be, nv, nu: (i, 0))),
        out_shape=jax.ShapeDtypeStruct((n_slots, Fd), BF16),
        compiler_params=_cp("arbitrary"),
        name="expert_up",
    )(block_e, n_valid, n_used, xs, w_g, w_u, b_g, b_u)


def _gm2_kernel(be_ref, nu_ref, a_ref, wd_ref, bd_ref, o_ref):
    i = pl.program_id(0)

    @pl.when(i < nu_ref[0])
    def _():
        o_ref[...] = _dot(a_ref[...], wd_ref[...]) + bd_ref[...]

    @pl.when(i >= nu_ref[0])
    def _():
        o_ref[...] = jnp.zeros_like(o_ref)


def _gm2(act, block_e, n_used, layer, w_d, b_d):
    n_slots, Fd = act.shape
    D = w_d.shape[-1]
    nb = n_slots // MOE_BLOCK
    return pl.pallas_call(
        _gm2_kernel,
        grid_spec=pltpu.PrefetchScalarGridSpec(
            num_scalar_prefetch=2, grid=(nb,),
            in_specs=[pl.BlockSpec((MOE_BLOCK, Fd), lambda i, be, nu: (i, 0)),
                      pl.BlockSpec((None, None, Fd, D), lambda i, be, nu: (layer, be[i], 0, 0)),
                      pl.BlockSpec((None, 1, D), lambda i, be, nu: (be[i], 0, 0))],
            out_specs=pl.BlockSpec((MOE_BLOCK, D), lambda i, be, nu: (i, 0))),
        out_shape=jax.ShapeDtypeStruct((n_slots, D), F32),
        compiler_params=_cp("arbitrary"),
        name="expert_down",
    )(block_e, n_used, act, w_d, b_d)


def _combine_kernel(dest_ref, x_ref, tw_ref, g2_ref, y_ref, o_ref, buf, sem):
    tm = x_ref.shape[0]

    def row_copy(r, k, d):
        return pltpu.make_async_copy(y_ref.at[pl.ds(d, 1), :], buf.at[k, pl.ds(r, 1), :], sem)

    def issue(r, c):
        for k in range(TOP_K):
            row_copy(r, k, dest_ref[r * TOP_K + k]).start()
        return c

    lax.fori_loop(0, tm, issue, 0)

    def drain(r, c):
        row_copy(0, 0, 0).wait()
        return c

    lax.fori_loop(0, tm * TOP_K, drain, 0, unroll=8)
    tw = tw_ref[...]
    acc = tw[:, 0:1] * buf[0]
    for k in range(1, TOP_K):
        acc = acc + tw[:, k:k + 1] * buf[k]
    o_ref[...] = x_ref[...] + g2_ref[...] * acc


def _combine(x, tw, mod, b0, y_slots, dest):
    B, S, D = x.shape
    tm = min(256, S)
    spb = S // tm
    row = lambda w: pl.BlockSpec((None, tm, w), lambda b, s: (b, s, 0))
    return pl.pallas_call(
        _combine_kernel,
        grid=(B, spb),
        in_specs=[pl.BlockSpec((tm * TOP_K,), lambda b, s: (b * spb + s,), memory_space=pltpu.SMEM),
                  row(D), row(LANES), _mod_spec(D, b0, 5),
                  pl.BlockSpec(memory_space=pl.ANY)],
        out_specs=row(D),
        out_shape=jax.ShapeDtypeStruct((B, S, D), F32),
        scratch_shapes=[pltpu.VMEM((TOP_K, tm, D), F32), pltpu.SemaphoreType.DMA(())],
        compiler_params=_cp("arbitrary", "arbitrary"),
        name="combine",
    )(dest, x, tw, mod, y_slots)


def _slot_tables(te, rk, cnt, n_blocks, E):
    counts = cnt[0, :E]
    padded = (counts + MOE_BLOCK - 1) // MOE_BLOCK * MOE_BLOCK
    pad_end = jnp.cumsum(padded)
    pad_start = pad_end - padded
    experts = jnp.arange(E, dtype=jnp.int32)

    def lookup(table, idx):
        return jnp.sum(jnp.where(idx[..., None] == experts, table, 0), axis=-1)

    dest = (lookup(pad_start, te) + rk).reshape(-1).astype(jnp.int32)
    blk0 = jnp.arange(n_blocks, dtype=jnp.int32) * MOE_BLOCK
    block_e = jnp.minimum(jnp.sum(pad_end[None, :] <= blk0[:, None], axis=1), E - 1).astype(jnp.int32)
    n_valid = jnp.clip(lookup(counts, block_e) - (blk0 - lookup(pad_start, block_e)), 0, MOE_BLOCK).astype(jnp.int32)
    n_used = (pad_end[-1:] // MOE_BLOCK).astype(jnp.int32)
    return dest, block_e, n_valid, n_used


def _layer(x, b0, mod, tabs, band, p):
    B, S, D = x.shape
    N = B * S
    E = p['w_g'].shape[1]
    h = _norm_mod(x, p['norm_mix_g'], mod, b0, 1, 0)
    q, k = _proj_qk(h, p['w_qk'], p['q_norm_g'], p['k_norm_g'], tabs)
    vt, u, qn, kn, vn = _proj_mid(h, p['w_mid'], p['w_vt'])
    gates = _proj_gate(h.reshape(N, D), p['w_gate']).reshape(B, S, 3 * D)
    o_attn = _attention(q, k, vt)
    o_pool = _pool(u, band, p['w_pool'], p['pool_scale'])
    o_na = _na(qn, kn, vn, p['na_bias'])
    x = _merge(o_attn, o_pool, o_na, gates, x, mod, b0, p['w_br_attn'], p['w_br_pool'], p['w_br_na'], p['w_out'])
    h2, te, tw, rk, cnt = _ffn_prep(x, p['norm_ffn_g'], mod, b0, p['wr_hi'], p['wr_lo'], p['br'])
    n_blocks = -(-N * TOP_K // MOE_BLOCK) + E
    dest, block_e, n_valid, n_used = _slot_tables(te[..., :TOP_K], rk[..., :TOP_K], cnt, n_blocks, E)
    xs = _dispatch(h2.reshape(N, D), dest, n_valid)
    act = _gm1(xs, block_e, n_valid, n_used, p['layer'], p['w_g'], p['w_u'], p['b_g'], p['b_u'])
    y_slots = _gm2(act, block_e, n_used, p['layer'], p['w_d'], p['b_d'])
    return _combine(x, tw, mod, b0, y_slots, dest)


def kernel(x_prompt, x_sample, c_prompt, c_sample, w_ada, b_ada, norm_mix_g, norm_ffn_g, w_in, q_norm_g, k_norm_g, w_pool, pool_scale, rpb, w_br_attn, w_br_pool, w_br_na, w_out, w_router, b_router, w_gu, b_gu, w_down, b_down, final_g):
    L, D, _ = w_ada.shape
    E = w_router.shape[-1]
    Fd = w_down.shape[2]
    Bp, Bs = c_prompt.shape[0], c_sample.shape[0]
    n_c = Bp + Bs
    c_all = jnp.zeros((-(-n_c // 8) * 8, D), F32).at[:n_c].set(jnp.concatenate([c_prompt, c_sample], axis=0))
    mod = _ada(c_all, w_ada, b_ada).reshape(L, c_all.shape[0], 6, 1, D)

    band = jnp.asarray(_pool_band(), BF16)
    w_g, w_u = _split_gu(w_gu)
    w_d = w_down.astype(BF16)
    b_gu2 = b_gu.reshape(L, E, Fd, 2)
    qk_w = ATTN_W + KV_W
    mid_w = sum(MID_SECTIONS)
    wr = jnp.zeros((L, D, LANES), F32).at[:, :, :E].set(w_router)
    wr_hi = wr.astype(BF16)
    wr_lo = (wr - wr_hi.astype(F32)).astype(BF16)
    br = jnp.full((L, 1, LANES), NEG, F32).at[:, 0, :E].set(b_router)

    layers = []
    for l in range(L):
        layers.append(dict(
            norm_mix_g=norm_mix_g[l], norm_ffn_g=norm_ffn_g[l],
            q_norm_g=q_norm_g[l], k_norm_g=k_norm_g[l],
            w_qk=w_in[l, :, :qk_w].astype(BF16),
            w_vt=w_in[l, :, qk_w:qk_w + KV_W].T.astype(BF16),
            w_mid=w_in[l, :, qk_w + KV_W:qk_w + KV_W + mid_w].astype(BF16),
            w_gate=w_in[l, :, qk_w + KV_W + mid_w:].astype(BF16),
            w_pool=w_pool[l].astype(BF16), pool_scale=pool_scale[l],
            na_bias=_na_bias(rpb[l]),
            w_br_attn=w_br_attn[l].astype(BF16), w_br_pool=w_br_pool[l].astype(BF16),
            w_br_na=w_br_na[l].astype(BF16), w_out=w_out[l].astype(BF16),
            wr_hi=wr_hi[l], wr_lo=wr_lo[l], br=br[l],
            layer=l, w_g=w_g, w_u=w_u,
            b_g=b_gu2[l, :, :, 0].reshape(E, 1, Fd), b_u=b_gu2[l, :, :, 1].reshape(E, 1, Fd),
            w_d=w_d, b_d=b_down[l].reshape(E, 1, D),
        ))

    outs = []
    for x, b0 in ((x_prompt, 0), (x_sample, Bp)):
        tabs = _rope_tables(x.shape[1])
        for l in range(L):
            x = _layer(x, b0, mod[l], tabs, band, layers[l])
        outs.append(_final_norm(x, final_g))
    return tuple(outs)
```

```python
import functools
import math

import numpy as np
import jax
import jax.numpy as jnp
from jax import lax
from jax.experimental import pallas as pl
from jax.experimental.pallas import tpu as pltpu

HEAD_DIM = 128
N_Q_HEADS = 8
N_KV_HEADS = 2
GQA_GROUP = N_Q_HEADS // N_KV_HEADS
ATTN_W = N_Q_HEADS * HEAD_DIM
KV_W = N_KV_HEADS * HEAD_DIM
GRID_W = 64
ROPE_THETA = 10000.0
ROPE_FREQS = HEAD_DIM // 4
POOL_WINDOWS = (2, 4, 8, 16)
N_POOL = len(POOL_WINDOWS)
POOL_DG = 128
POOL_W = N_POOL * POOL_DG
NA_HEADS = 4
NA_W = NA_HEADS * HEAD_DIM
NA_ROWS = 8
NA_COLS = 16
TOP_K = 4
MOE_BLOCK = 256
HALF_BLOCK = MOE_BLOCK // 2
SWIGLU_ALPHA = 1.702
SWIGLU_LIMIT = 7.0
EPS = 1e-6

LANES = 128
POOL_HALO = 16
NEG = -1e30
VMEM_LIMIT = 56 * 1024 * 1024

F32 = jnp.float32
BF16 = jnp.bfloat16
_dot = functools.partial(jnp.dot, preferred_element_type=jnp.float32)


def _dot_nt(a, b):
    return lax.dot_general(a, b, (((1,), (1,)), ((), ())), preferred_element_type=jnp.float32)


def _cp(*sem, **kw):
    return pltpu.CompilerParams(dimension_semantics=sem, vmem_limit_bytes=VMEM_LIMIT, **kw)


def _split_bf16(a):
    hi = a.astype(BF16)
    lo = (a - hi.astype(F32)).astype(BF16)
    return hi, lo


def _resident(shape, index_map):
    return pl.BlockSpec(shape, index_map, pipeline_mode=pl.Buffered(1))


def _ada_kernel(c_ref, w_ref, b_ref, o_ref):
    c = c_ref[...]
    a = c * jax.nn.sigmoid(c)
    ah, al = _split_bf16(a)
    wh, wl = _split_bf16(w_ref[...])
    o_ref[...] = _dot(ah, wh) + _dot(al, wh) + _dot(ah, wl) + b_ref[...]


def _ada(c_all, w_ada, b_ada):
    L, D, D6 = w_ada.shape
    Bp = c_all.shape[0]
    tn = math.gcd(1024, D6)
    return pl.pallas_call(
        _ada_kernel,
        grid=(L, D6 // tn),
        in_specs=[pl.BlockSpec((Bp, D), lambda l, j: (0, 0)),
                  pl.BlockSpec((None, D, tn), lambda l, j: (l, 0, j)),
                  pl.BlockSpec((None, 1, tn), lambda l, j: (l, 0, j))],
        out_specs=pl.BlockSpec((None, Bp, tn), lambda l, j: (l, 0, j)),
        out_shape=jax.ShapeDtypeStruct((L, Bp, D6), F32),
        compiler_params=_cp("parallel", "parallel"),
        name="ada",
    )(c_all, w_ada, b_ada.reshape(L, 1, D6))


def _rms_mod(x, g, sc, sh):
    r = lax.rsqrt(jnp.mean(x * x, axis=-1, keepdims=True) + EPS)
    return (x * r * g) * (1.0 + sc) + sh


def _norm_mod_kernel(x_ref, g_ref, sc_ref, sh_ref, o_ref):
    o_ref[...] = _rms_mod(x_ref[...], g_ref[...], sc_ref[...], sh_ref[...]).astype(o_ref.dtype)


def _mod_spec(D, b0, which):
    return pl.BlockSpec((None, None, 1, D), lambda b, s: (b + b0, which, 0, 0))


def _norm_mod(x, g, mod, b0, i_sc, i_sh):
    B, S, D = x.shape
    ts = min(512, S)
    return pl.pallas_call(
        _norm_mod_kernel,
        grid=(B, S // ts),
        in_specs=[pl.BlockSpec((None, ts, D), lambda b, s: (b, s, 0)),
                  pl.BlockSpec((1, D), lambda b, s: (0, 0)),
                  _mod_spec(D, b0, i_sc), _mod_spec(D, b0, i_sh)],
        out_specs=pl.BlockSpec((None, ts, D), lambda b, s: (b, s, 0)),
        out_shape=jax.ShapeDtypeStruct((B, S, D), BF16),
        compiler_params=_cp("parallel", "parallel"),
        name="norm_mod",
    )(x, g.reshape(1, D), mod, mod)


def _final_norm_kernel(x_ref, g_ref, o_ref):
    x = x_ref[...]
    r = lax.rsqrt(jnp.mean(x * x, axis=-1, keepdims=True) + EPS)
    o_ref[...] = x * r * g_ref[...]


def _final_norm(x, g):
    B, S, D = x.shape
    ts = min(512, S)
    return pl.pallas_call(
        _final_norm_kernel,
        grid=(B, S // ts),
        in_specs=[pl.BlockSpec((None, ts, D), lambda b, s: (b, s, 0)),
                  pl.BlockSpec((1, D), lambda b, s: (0, 0))],
        out_specs=pl.BlockSpec((None, ts, D), lambda b, s: (b, s, 0)),
        out_shape=jax.ShapeDtypeStruct((B, S, D), F32),
        compiler_params=_cp("parallel", "parallel"),
        name="final_norm",
    )(x, g.reshape(1, D))


def _rope_tables(S):
    t = jnp.arange(S)
    pos = jnp.stack([t // GRID_W, t % GRID_W], axis=-1).astype(F32)
    inv = ROPE_THETA ** (-jnp.arange(ROPE_FREQS, dtype=F32) / ROPE_FREQS)
    ang = pos[..., None] * inv
    cos, sin = jnp.cos(ang), jnp.sin(ang)
    zero = jnp.zeros_like(sin)
    c = jnp.stack([cos, cos], axis=2).reshape(S, HEAD_DIM)
    sa = jnp.stack([-sin, zero], axis=2).reshape(S, HEAD_DIM)
    sb = jnp.stack([zero, sin], axis=2).reshape(S, HEAD_DIM)
    return c, sa, sb


def _proj_qk_kernel(a_ref, w_ref, gq_ref, gk_ref, c_ref, sa_ref, sb_ref, q_ref, k_ref):
    acc = _dot(a_ref[...], w_ref[...])
    c, sa, sb = c_ref[...], sa_ref[...], sb_ref[...]
    scale = math.log2(math.e) / math.sqrt(HEAD_DIM)
    heads = range(N_Q_HEADS + N_KV_HEADS)
    xs = [acc[:, hh * HEAD_DIM:(hh + 1) * HEAD_DIM] for hh in heads]
    rs = [lax.rsqrt(jnp.mean(x * x, axis=-1, keepdims=True) + EPS) for x in xs]
    xn = [xs[hh] * rs[hh] * (gq_ref[...] if hh < N_Q_HEADS else gk_ref[...]) for hh in heads]
    up = [pltpu.roll(x, HEAD_DIM - ROPE_FREQS, 1) for x in xn]
    dn = [pltpu.roll(x, ROPE_FREQS, 1) for x in xn]
    for hh in heads:
        y = xn[hh] * c + up[hh] * sa + dn[hh] * sb
        if hh < N_Q_HEADS:
            q_ref[:, hh * HEAD_DIM:(hh + 1) * HEAD_DIM] = (y * scale).astype(q_ref.dtype)
        else:
            kk = hh - N_Q_HEADS
            k_ref[:, kk * HEAD_DIM:(kk + 1) * HEAD_DIM] = y.astype(k_ref.dtype)


def _proj_qk(h, w_qk, gq, gk, tabs):
    B, S, D = h.shape
    tm = min(1024, S)
    W = ATTN_W + KV_W
    tab_spec = pl.BlockSpec((tm, HEAD_DIM), lambda b, s: (s, 0))
    vec_spec = pl.BlockSpec((1, HEAD_DIM), lambda b, s: (0, 0))
    return pl.pallas_call(
        _proj_qk_kernel,
        grid=(B, S // tm),
        in_specs=[pl.BlockSpec((None, tm, D), lambda b, s: (b, s, 0)),
                  _resident((D, W), lambda b, s: (0, 0)),
                  vec_spec, vec_spec, tab_spec, tab_spec, tab_spec],
        out_specs=[pl.BlockSpec((None, tm, ATTN_W), lambda b, s: (b, s, 0)),
                   pl.BlockSpec((None, tm, KV_W), lambda b, s: (b, s, 0))],
        out_shape=[jax.ShapeDtypeStruct((B, S, ATTN_W), BF16),
                   jax.ShapeDtypeStruct((B, S, KV_W), BF16)],
        compiler_params=_cp("parallel", "parallel"),
        name="proj_qk",
    )(h, w_qk, gq.reshape(1, HEAD_DIM), gk.reshape(1, HEAD_DIM), *tabs)


MID_SECTIONS = (POOL_W, NA_W, NA_W, NA_W)
ATTN_TK = 512


def _proj_mid_kernel(a_ref, w_ref, wvt_ref, vt_ref, *o_refs):
    a = a_ref[...]
    acc = _dot(a, w_ref[...])
    off = 0
    for o_ref, width in zip(o_refs, MID_SECTIONS):
        o_ref[...] = acc[:, off:off + width].astype(o_ref.dtype)
        off += width
    tk = vt_ref.shape[-1]
    for c in range(vt_ref.shape[0]):
        vt_ref[c] = _dot_nt(wvt_ref[...], a[c * tk:(c + 1) * tk]).astype(vt_ref.dtype)


def _proj_mid(h, w_mid, w_vt):
    B, S, D = h.shape
    tm = min(1024, S)
    tk = min(ATTN_TK, S)
    W = sum(MID_SECTIONS)
    return pl.pallas_call(
        _proj_mid_kernel,
        grid=(B, S // tm),
        in_specs=[pl.BlockSpec((None, tm, D), lambda b, s: (b, s, 0)),
                  _resident((D, W), lambda b, s: (0, 0)),
                  _resident((KV_W, D), lambda b, s: (0, 0))],
        out_specs=[pl.BlockSpec((None, tm // tk, KV_W, tk), lambda b, s: (b, s, 0, 0))]
        + [pl.BlockSpec((None, tm, w), lambda b, s: (b, s, 0)) for w in MID_SECTIONS],
        out_shape=[jax.ShapeDtypeStruct((B, S // tk, KV_W, tk), BF16)]
        + [jax.ShapeDtypeStruct((B, S, w), BF16) for w in MID_SECTIONS],
        compiler_params=_cp("parallel", "parallel"),
        name="proj_mid",
    )(h, w_mid, w_vt)


def _proj_gate_kernel(a_ref, w_ref, o_ref):
    o_ref[...] = jax.nn.sigmoid(_dot(a_ref[...], w_ref[...])).astype(o_ref.dtype)


def _proj_gate(h, w_gate):
    N, D = h.shape
    W = w_gate.shape[1]
    tm = min(1024, N)
    tn = min(2048, D)
    return pl.pallas_call(
        _proj_gate_kernel,
        grid=(W // tn, N // tm),
        in_specs=[pl.BlockSpec((tm, D), lambda j, i: (i, 0)),
                  pl.BlockSpec((D, tn), lambda j, i: (0, j))],
        out_specs=pl.BlockSpec((tm, tn), lambda j, i: (i, j)),
        out_shape=jax.ShapeDtypeStruct((N, W), BF16),
        compiler_params=_cp("parallel", "parallel"),
        name="proj_gate",
    )(h, w_gate)


def _attn_kernel(q_ref, k_ref, vt_ref, o_ref, s_ref):
    n_chunks, _, tk = vt_ref.shape
    tq = q_ref.shape[0]
    heads = range(GQA_GROUP)

    def scores(j, slot):
        k = k_ref[pl.ds(pl.multiple_of(j * tk, tk), tk), :]
        for h in heads:
            s_ref[slot, h] = _dot_nt(k, q_ref[:, h * HEAD_DIM:(h + 1) * HEAD_DIM])

    def consume(j, slot, carry):
        vt = vt_ref[j]
        m_new = [jnp.maximum(carry[h][0], jnp.max(s_ref[slot, h], axis=0, keepdims=True)) for h in heads]
        alpha = [jnp.exp2(carry[h][0] - m_new[h]) for h in heads]
        p = [jnp.exp2(s_ref[slot, h] - m_new[h]) for h in heads]
        l = [alpha[h] * carry[h][1] + jnp.sum(p[h], axis=0, keepdims=True) for h in heads]
        pv = [_dot(vt, p[h].astype(BF16)) for h in heads]
        return tuple((m_new[h], l[h], alpha[h] * carry[h][2] + pv[h]) for h in heads)

    init = tuple((jnp.full((1, tq), NEG, F32), jnp.zeros((1, tq), F32), jnp.zeros((HEAD_DIM, tq), F32))
                 for _ in heads)
    scores(0, 0)
    if n_chunks % 2:
        def body(j, carry):
            carry = consume(j, 0, carry)
            scores(jnp.minimum(j + 1, n_chunks - 1), 0)
            return carry
        final = lax.fori_loop(0, n_chunks, body, init)
    else:
        def body(jj, carry):
            j = 2 * jj
            scores(j + 1, 1)
            carry = consume(j, 0, carry)
            scores(jnp.minimum(j + 2, n_chunks - 1), 0)
            return consume(j + 1, 1, carry)
        final = lax.fori_loop(0, n_chunks // 2, body, init)
    for h in heads:
        m, l, acc = final[h]
        o_ref[:, h * HEAD_DIM:(h + 1) * HEAD_DIM] = (acc / l).T.astype(o_ref.dtype)


def _attention(q, k, vt):
    B, S, _ = q.shape
    n_chunks, _, tk = vt.shape[1:]
    tq = min(512, S)
    gw = GQA_GROUP * HEAD_DIM
    return pl.pallas_call(
        _attn_kernel,
        grid=(B, N_KV_HEADS, S // tq),
        in_specs=[pl.BlockSpec((None, tq, gw), lambda b, g, i: (b, i, g)),
                  pl.BlockSpec((None, S, HEAD_DIM), lambda b, g, i: (b, 0, g)),
                  pl.BlockSpec((None, n_chunks, HEAD_DIM, tk), lambda b, g, i: (b, 0, g, 0))],
        out_specs=pl.BlockSpec((None, tq, gw), lambda b, g, i: (b, i, g)),
        out_shape=jax.ShapeDtypeStruct((B, S, ATTN_W), BF16),
        scratch_shapes=[pltpu.VMEM((2, GQA_GROUP, tk, tq), F32)],
        compiler_params=_cp("parallel", "parallel", "parallel"),
        name="gqa",
    )(q, k, vt)


def _pool_band():
    T = MOE_BLOCK
    i = np.arange(T)[:, None] + POOL_HALO
    j = np.arange(T + 2 * POOL_HALO)[None, :]
    return np.stack([((j >= i - w // 2) & (j < i + w // 2)) for w in POOL_WINDOWS]).astype(np.float32)


def _pool_kernel(u_ref, band_ref, wp_ref, ps_ref, o_ref, *, T):
    S = u_ref.shape[0]
    nc = S // T
    halo = jnp.zeros((POOL_HALO, POOL_DG), BF16)
    for g in range(N_POOL):
        hw = POOL_WINDOWS[g] // 2
        cols = slice(g * POOL_DG, (g + 1) * POOL_DG)
        for c in range(nc):
            t0 = c * T
            mid = u_ref[t0:t0 + T, cols]
            prev = u_ref[t0 - POOL_HALO:t0, cols] if c > 0 else halo
            nxt = u_ref[t0 + T:t0 + T + POOL_HALO, cols] if c < nc - 1 else halo
            win = _dot(band_ref[g], jnp.concatenate([prev, mid, nxt], axis=0))
            t = t0 + lax.broadcasted_iota(jnp.int32, (T, POOL_DG), 0)
            cnt = (jnp.minimum(t + hw, S) - jnp.maximum(t - hw, 0)).astype(F32)
            pooled = win / cnt - mid.astype(F32)
            mixed = _dot(pooled.astype(BF16), wp_ref[g]) * ps_ref[:, cols]
            o_ref[t0:t0 + T, cols] = mixed.astype(o_ref.dtype)


def _pool(u, band, w_pool, pool_scale):
    B, S, _ = u.shape
    T = band.shape[1]
    return pl.pallas_call(
        functools.partial(_pool_kernel, T=T),
        grid=(B,),
        in_specs=[pl.BlockSpec((None, S, POOL_W), lambda b: (b, 0, 0)),
                  pl.BlockSpec(band.shape, lambda b: (0, 0, 0)),
                  pl.BlockSpec(w_pool.shape, lambda b: (0, 0, 0)),
                  pl.BlockSpec((1, POOL_W), lambda b: (0, 0))],
        out_specs=pl.BlockSpec((None, S, POOL_W), lambda b: (b, 0, 0)),
        out_shape=jax.ShapeDtypeStruct((B, S, POOL_W), BF16),
        compiler_params=_cp("parallel"),
        name="pool",
    )(u, band, w_pool, pool_scale.reshape(1, POOL_W))


def _na_bias(rpb):
    H = rpb.shape[0]
    nr, nc, P = 2 * NA_ROWS - 1, 2 * NA_COLS - 1, 2 * GRID_W
    w = jnp.full((H, nr, P), NEG, F32).at[:, :, GRID_W - NA_COLS:GRID_W - 1 + NA_COLS].set(rpb.reshape(H, nr, nc))
    skew = jnp.tile(w, (1, 1, GRID_W))[:, :, :GRID_W * (P - 1)].reshape(H, nr, GRID_W, P - 1)
    toe = skew[..., GRID_W - 1:2 * GRID_W - 1]
    qc = np.arange(GRID_W)[:, None]
    kc = np.arange(GRID_W)[None, :]
    cs = np.clip(qc - NA_COLS // 2, 0, GRID_W - NA_COLS)
    toe = jnp.where((kc >= cs) & (kc < cs + NA_COLS), toe, NEG)
    per_off = jnp.stack([toe[:, NA_ROWS - 1 - o:2 * NA_ROWS - 1 - o] for o in range(NA_ROWS)], axis=1)
    return per_off.transpose(0, 1, 3, 2, 4).reshape(H, NA_ROWS, GRID_W, NA_ROWS * GRID_W)


def _na_kernel(q_ref, k_ref, v_ref, b_ref, o_ref, *, rb, R):
    i = pl.program_id(1)
    scale = 1.0 / math.sqrt(HEAD_DIM)
    win = NA_ROWS * GRID_W
    def window(rr):
        r = i * rb + rr
        rs = jnp.clip(r - NA_ROWS // 2, 0, R - NA_ROWS)
        return r - rs, pl.multiple_of(rs * GRID_W, GRID_W)

    def scores(rr, h):
        off, start = window(rr)
        cols = slice(h * HEAD_DIM, (h + 1) * HEAD_DIM)
        q = q_ref[rr * GRID_W:(rr + 1) * GRID_W, cols]
        return _dot_nt(q, k_ref[pl.ds(start, win), cols]) * scale + b_ref[h, off]

    def finish(rr, h, s):
        _, start = window(rr)
        cols = slice(h * HEAD_DIM, (h + 1) * HEAD_DIM)
        m = jnp.max(s, axis=-1, keepdims=True)
        p = jnp.exp(s - m)
        l = jnp.sum(p, axis=-1, keepdims=True)
        out = _dot(p.astype(BF16), v_ref[pl.ds(start, win), cols]) / l
        o_ref[rr * GRID_W:(rr + 1) * GRID_W, cols] = out.astype(o_ref.dtype)

    todo = [(rr, h) for rr in range(rb) for h in range(NA_HEADS)]
    s_next = scores(*todo[0])
    for n, (rr, h) in enumerate(todo):
        s_cur = s_next
        if n + 1 < len(todo):
            s_next = scores(*todo[n + 1])
        finish(rr, h, s_cur)


def _na(q, k, v, bias):
    B, S, _ = q.shape
    R = S // GRID_W
    assert R >= NA_ROWS
    rb = 8
    return pl.pallas_call(
        functools.partial(_na_kernel, rb=rb, R=R),
        grid=(B, R // rb),
        in_specs=[pl.BlockSpec((None, rb * GRID_W, NA_W), lambda b, i: (b, i, 0)),
                  pl.BlockSpec((None, S, NA_W), lambda b, i: (b, 0, 0)),
                  pl.BlockSpec((None, S, NA_W), lambda b, i: (b, 0, 0)),
                  pl.BlockSpec(bias.shape, lambda b, i: (0, 0, 0, 0))],
        out_specs=pl.BlockSpec((None, rb * GRID_W, NA_W), lambda b, i: (b, i, 0)),
        out_shape=jax.ShapeDtypeStruct((B, S, NA_W), BF16),
        compiler_params=_cp("parallel", "parallel"),
        name="natten",
    )(q, k, v, bias)


def _merge_kernel(oa_ref, op_ref, on_ref, ga_ref, gp_ref, gn_ref, x_ref, g1_ref,
                  wa_ref, wp_ref, wn_ref, wo_ref, o_ref):
    m = ga_ref[...].astype(F32) * _dot(oa_ref[...], wa_ref[...])
    m = m + gp_ref[...].astype(F32) * _dot(op_ref[...], wp_ref[...])
    m = m + gn_ref[...].astype(F32) * _dot(on_ref[...], wn_ref[...])
    y = _dot(m.astype(BF16), wo_ref[...])
    o_ref[...] = x_ref[...] + g1_ref[...] * y


def _merge(o_attn, o_pool, o_na, gates, x, mod, b0, w_a, w_p, w_n, w_o):
    B, S, D = x.shape
    tm = min(512, S)
    row = lambda w: pl.BlockSpec((None, tm, w), lambda b, s: (b, s, 0))
    gate = lambda j: pl.BlockSpec((None, tm, D), lambda b, s: (b, s, j))
    res = lambda w: _resident(w.shape, lambda b, s: (0, 0))
    return pl.pallas_call(
        _merge_kernel,
        grid=(B, S // tm),
        in_specs=[row(ATTN_W), row(POOL_W), row(NA_W), gate(0), gate(1), gate(2), row(D),
                  _mod_spec(D, b0, 2), res(w_a), res(w_p), res(w_n), res(w_o)],
        out_specs=row(D),
        out_shape=jax.ShapeDtypeStruct((B, S, D), F32),
        compiler_params=_cp("parallel", "parallel"),
        name="merge",
    )(o_attn, o_pool, o_na, gates, gates, gates, x, mod, w_a, w_p, w_n, w_o)


def _ffn_prep_kernel(x_ref, g_ref, sc_ref, sh_ref, wrh_ref, wrl_ref, br_ref,
                     h_ref, te_ref, tw_ref, rk_ref, cnt_ref, carry_ref):
    first = jnp.logical_and(pl.program_id(0) == 0, pl.program_id(1) == 0)

    @pl.when(first)
    def _():
        carry_ref[...] = jnp.zeros_like(carry_ref)

    tm = x_ref.shape[0]
    h = _rms_mod(x_ref[...], g_ref[...], sc_ref[...], sh_ref[...])
    h_ref[...] = h
    hh, hl = _split_bf16(h)
    wrh = wrh_ref[...]
    logits = _dot(hh, wrh) + _dot(hl, wrh) + _dot(hh, wrl_ref[...]) + br_ref[...]
    lane = lax.broadcasted_iota(jnp.int32, (tm, LANES), 1).astype(F32)
    vals, idxs, hots = [], [], []
    cur = logits
    for _ in range(TOP_K):
        m = jnp.max(cur, axis=-1, keepdims=True)
        idx = jnp.min(jnp.where(cur == m, lane, float(LANES)), axis=-1, keepdims=True)
        hot = lane == idx
        vals.append(m)
        idxs.append(idx)
        hots.append(hot)
        cur = jnp.where(hot, -3e38, cur)
    ex = [jnp.exp(v - vals[0]) for v in vals]
    den = ex[0] + ex[1] + ex[2] + ex[3]
    onehot = sum(hh_.astype(F32) for hh_ in hots)
    ri = lax.broadcasted_iota(jnp.int32, (tm, tm), 0)
    ci = lax.broadcasted_iota(jnp.int32, (tm, tm), 1)
    tri = (ri > ci).astype(BF16)
    carry = carry_ref[0:1, :]
    before = _dot(tri, onehot.astype(BF16)) + carry
    te = jnp.zeros((tm, LANES), F32)
    tw = jnp.zeros((tm, LANES), F32)
    rk = jnp.zeros((tm, LANES), F32)
    for k in range(TOP_K):
        sel = lane == float(k)
        rank = jnp.sum(jnp.where(hots[k], before, 0.0), axis=-1, keepdims=True)
        te = jnp.where(sel, idxs[k], te)
        tw = jnp.where(sel, ex[k] / den, tw)
        rk = jnp.where(sel, rank, rk)
    te_ref[...] = te.astype(jnp.int32)
    tw_ref[...] = tw
    rk_ref[...] = rk.astype(jnp.int32)
    total = carry + jnp.sum(onehot, axis=0, keepdims=True)
    carry_ref[...] = jnp.broadcast_to(total, carry_ref.shape)
    cnt_ref[...] = jnp.broadcast_to(total, cnt_ref.shape).astype(jnp.int32)


def _ffn_prep(x, g, mod, b0, wr_hi, wr_lo, br):
    B, S, D = x.shape
    tm = min(512, S)
    row = lambda w: pl.BlockSpec((None, tm, w), lambda b, s: (b, s, 0))
    const = lambda shape: pl.BlockSpec(shape, lambda b, s: (0, 0))
    return pl.pallas_call(
        _ffn_prep_kernel,
        grid=(B, S // tm),
        in_specs=[row(D), const((1, D)), _mod_spec(D, b0, 4), _mod_spec(D, b0, 3),
                  const((D, LANES)), const((D, LANES)), const((1, LANES))],
        out_specs=[row(D), row(LANES), row(LANES), row(LANES), const((8, LANES))],
        out_shape=[jax.ShapeDtypeStruct((B, S, D), F32),
                   jax.ShapeDtypeStruct((B, S, LANES), jnp.int32),
                   jax.ShapeDtypeStruct((B, S, LANES), F32),
                   jax.ShapeDtypeStruct((B, S, LANES), jnp.int32),
                   jax.ShapeDtypeStruct((8, LANES), jnp.int32)],
        scratch_shapes=[pltpu.VMEM((8, LANES), F32)],
        compiler_params=_cp("arbitrary", "arbitrary"),
        name="ffn_prep",
    )(x, g.reshape(1, D), mod, mod, wr_hi, wr_lo, br)


def _dispatch_kernel(nv_ref, dest_ref, h_ref, xs_ref, zeros, sem, zsem):
    tm = h_ref.shape[0]
    nb = nv_ref.shape[0]

    @pl.when(pl.program_id(0) == 0)
    def _():
        zeros[...] = jnp.zeros_like(zeros)

        def tail_copies(b, fn):
            nv = nv_ref[b]
            base = b * MOE_BLOCK
            up = (nv + 7) // 8 * 8
            for r in range(7):
                @pl.when(nv + r < up)
                def _(r=r):
                    fn(pltpu.make_async_copy(zeros.at[pl.ds(0, 1), :], xs_ref.at[pl.ds(base + nv + r, 1), :], zsem))
            units = (MOE_BLOCK - up) // 8
            off = up
            for bit in (32, 16, 8, 4, 2, 1):
                size = bit * 8
                hit = (units & bit) != 0

                @pl.when(hit)
                def _(size=size, off=off):
                    dst = xs_ref.at[pl.ds(pl.multiple_of(base + off, 8), size), :]
                    fn(pltpu.make_async_copy(zeros.at[pl.ds(0, size), :], dst, zsem))
                off = off + jnp.where(hit, size, 0)

        def start_all(b, c):
            tail_copies(b, lambda cp: cp.start())
            return c

        def wait_all(b, c):
            tail_copies(b, lambda cp: cp.wait())
            return c

        lax.fori_loop(0, nb, start_all, 0)
        lax.fori_loop(0, nb, wait_all, 0)

    def row_copy(r, d):
        return pltpu.make_async_copy(h_ref.at[pl.ds(r, 1), :], xs_ref.at[pl.ds(d, 1), :], sem)

    def issue(r, c):
        for k in range(TOP_K):
            row_copy(r, dest_ref[r * TOP_K + k]).start()
        return c

    lax.fori_loop(0, tm, issue, 0)

    def drain(r, c):
        row_copy(0, 0).wait()
        return c

    lax.fori_loop(0, tm * TOP_K, drain, 0, unroll=8)


def _dispatch(h, dest, n_valid):
    N, D = h.shape
    tm = min(512, N)
    n_slots = n_valid.shape[0] * MOE_BLOCK
    return pl.pallas_call(
        _dispatch_kernel,
        grid_spec=pltpu.PrefetchScalarGridSpec(
            num_scalar_prefetch=1, grid=(N // tm,),
            in_specs=[pl.BlockSpec((tm * TOP_K,), lambda i, nv: (i,), memory_space=pltpu.SMEM),
                      pl.BlockSpec((tm, D), lambda i, nv: (i, 0))],
            out_specs=pl.BlockSpec(memory_space=pl.ANY),
            scratch_shapes=[pltpu.VMEM((MOE_BLOCK, D), F32), pltpu.SemaphoreType.DMA(()),
                            pltpu.SemaphoreType.DMA(())]),
        out_shape=jax.ShapeDtypeStruct((n_slots, D), F32),
        compiler_params=_cp("arbitrary", has_side_effects=True),
        name="dispatch",
    )(n_valid, dest, h)


SPLIT_CHUNK = 2 * LANES


def _split_gu_kernel(w_ref, sel_ref, g_ref, u_ref):
    sel = sel_ref[...]
    for c in range(w_ref.shape[1] // SPLIT_CHUNK):
        chunk = w_ref[:, c * SPLIT_CHUNK:(c + 1) * SPLIT_CHUNK].astype(BF16)
        r = _dot(chunk, sel)
        g_ref[:, c * LANES:(c + 1) * LANES] = r[:, :LANES].astype(g_ref.dtype)
        u_ref[:, c * LANES:(c + 1) * LANES] = r[:, LANES:].astype(u_ref.dtype)


def _split_gu(w_gu):
    L, E, D, F2 = w_gu.shape
    Fd = F2 // 2
    tk = min(256, D)
    src = np.arange(SPLIT_CHUNK)[:, None]
    dst = np.arange(SPLIT_CHUNK)[None, :]
    sel = jnp.asarray(src == 2 * (dst % LANES) + dst // LANES, BF16)
    out_spec = pl.BlockSpec((None, None, tk, Fd), lambda l, e, k: (l, e, k, 0))
    return pl.pallas_call(
        _split_gu_kernel,
        grid=(L, E, D // tk),
        in_specs=[pl.BlockSpec((None, None, tk, F2), lambda l, e, k: (l, e, k, 0)),
                  pl.BlockSpec((SPLIT_CHUNK, SPLIT_CHUNK), lambda l, e, k: (0, 0))],
        out_specs=[out_spec, out_spec],
        out_shape=[jax.ShapeDtypeStruct((L, E, D, Fd), BF16)] * 2,
        compiler_params=_cp("parallel", "parallel", "parallel"),
        name="split_gate_up",
    )(w_gu, sel)


def _gm1_kernel(be_ref, nv_ref, nu_ref, x_ref, wg_ref, wu_ref, bg_ref, bu_ref, o_ref):
    i = pl.program_id(0)
    nv = nv_ref[i]
    used = i < nu_ref[0]

    def swiglu(n):
        rows = lax.broadcasted_iota(jnp.int32, (n, x_ref.shape[1]), 0)
        x = jnp.where(rows < nv, x_ref[0:n, :], 0.0).astype(BF16)
        g = _dot(x, wg_ref[...]) + bg_ref[...]
        u = _dot(x, wu_ref[...]) + bu_ref[...]
        gate = jnp.minimum(g, SWIGLU_LIMIT)
        up = jnp.clip(u, -SWIGLU_LIMIT, SWIGLU_LIMIT)
        return ((up + 1.0) * gate * jax.nn.sigmoid(SWIGLU_ALPHA * gate)).astype(o_ref.dtype)

    @pl.when(jnp.logical_and(used, nv > HALF_BLOCK))
    def _():
        o_ref[...] = swiglu(MOE_BLOCK)

    @pl.when(jnp.logical_and(used, nv <= HALF_BLOCK))
    def _():
        o_ref[0:HALF_BLOCK, :] = swiglu(HALF_BLOCK)
        o_ref[HALF_BLOCK:, :] = jnp.zeros((MOE_BLOCK - HALF_BLOCK, o_ref.shape[1]), o_ref.dtype)

    @pl.when(jnp.logical_not(used))
    def _():
        o_ref[...] = jnp.zeros_like(o_ref)


def _gm1(xs, block_e, n_valid, n_used, layer, w_g, w_u, b_g, b_u):
    n_slots, D = xs.shape
    Fd = w_g.shape[-1]
    nb = n_slots // MOE_BLOCK
    wspec = pl.BlockSpec((None, None, D, Fd), lambda i, be, nv, nu: (layer, be[i], 0, 0))
    bspec = pl.BlockSpec((None, 1, Fd), lambda i, be, nv, nu: (be[i], 0, 0))
    return pl.pallas_call(
        _gm1_kernel,
        grid_spec=pltpu.PrefetchScalarGridSpec(
            num_scalar_prefetch=3, grid=(nb,),
            in_specs=[pl.BlockSpec((MOE_BLOCK, D), lambda i, be, nv, nu: (i, 0)), wspec, wspec, bspec, bspec],
            out_specs=pl.BlockSpec((MOE_BLOCK, Fd), lambda i, be, nv, nu: (i, 0))),
        out_shape=jax.ShapeDtypeStruct((n_slots, Fd), BF16),
        compiler_params=_cp("arbitrary"),
        name="expert_up",
    )(block_e, n_valid, n_used, xs, w_g, w_u, b_g, b_u)


def _gm2_kernel(be_ref, nv_ref, nu_ref, a_ref, wd_ref, bd_ref, o_ref):
    i = pl.program_id(0)
    nv = nv_ref[i]
    used = i < nu_ref[0]

    @pl.when(jnp.logical_and(used, nv > HALF_BLOCK))
    def _():
        o_ref[...] = _dot(a_ref[...], wd_ref[...]) + bd_ref[...]

    @pl.when(jnp.logical_and(used, nv <= HALF_BLOCK))
    def _():
        o_ref[0:HALF_BLOCK, :] = _dot(a_ref[0:HALF_BLOCK, :], wd_ref[...]) + bd_ref[...]
        o_ref[HALF_BLOCK:, :] = jnp.zeros((MOE_BLOCK - HALF_BLOCK, o_ref.shape[1]), o_ref.dtype)

    @pl.when(jnp.logical_not(used))
    def _():
        o_ref[...] = jnp.zeros_like(o_ref)


def _gm2(act, block_e, n_valid, n_used, layer, w_d, b_d):
    n_slots, Fd = act.shape
    D = w_d.shape[-1]
    nb = n_slots // MOE_BLOCK
    return pl.pallas_call(
        _gm2_kernel,
        grid_spec=pltpu.PrefetchScalarGridSpec(
            num_scalar_prefetch=3, grid=(nb,),
            in_specs=[pl.BlockSpec((MOE_BLOCK, Fd), lambda i, be, nv, nu: (i, 0)),
                      pl.BlockSpec((None, None, Fd, D), lambda i, be, nv, nu: (layer, be[i], 0, 0)),
                      pl.BlockSpec((None, 1, D), lambda i, be, nv, nu: (be[i], 0, 0))],
            out_specs=pl.BlockSpec((MOE_BLOCK, D), lambda i, be, nv, nu: (i, 0))),
        out_shape=jax.ShapeDtypeStruct((n_slots, D), F32),
        compiler_params=_cp("arbitrary"),
        name="expert_down",
    )(block_e, n_valid, n_used, act, w_d, b_d)


def _combine_kernel(dest_ref, x_ref, tw_ref, g2_ref, y_ref, o_ref, buf, sem):
    tm = x_ref.shape[0]

    def row_copy(r, k, d):
        return pltpu.make_async_copy(y_ref.at[pl.ds(d, 1), :], buf.at[k, pl.ds(r, 1), :], sem)

    def issue(r, c):
        for k in range(TOP_K):
            row_copy(r, k, dest_ref[r * TOP_K + k]).start()
        return c

    lax.fori_loop(0, tm, issue, 0)

    def drain(r, c):
        row_copy(0, 0, 0).wait()
        return c

    lax.fori_loop(0, tm * TOP_K, drain, 0, unroll=8)
    tw = tw_ref[...]
    acc = tw[:, 0:1] * buf[0]
    for k in range(1, TOP_K):
        acc = acc + tw[:, k:k + 1] * buf[k]
    o_ref[...] = x_ref[...] + g2_ref[...] * acc


def _combine(x, tw, mod, b0, y_slots, dest):
    B, S, D = x.shape
    tm = min(256, S)
    spb = S // tm
    row = lambda w: pl.BlockSpec((None, tm, w), lambda b, s: (b, s, 0))
    return pl.pallas_call(
        _combine_kernel,
        grid=(B, spb),
        in_specs=[pl.BlockSpec((tm * TOP_K,), lambda b, s: (b * spb + s,), memory_space=pltpu.SMEM),
                  row(D), row(LANES), _mod_spec(D, b0, 5),
                  pl.BlockSpec(memory_space=pl.ANY)],
        out_specs=row(D),
        out_shape=jax.ShapeDtypeStruct((B, S, D), F32),
        scratch_shapes=[pltpu.VMEM((TOP_K, tm, D), F32), pltpu.SemaphoreType.DMA(())],
        compiler_params=_cp("arbitrary", "arbitrary"),
        name="combine",
    )(dest, x, tw, mod, y_slots)


def _slot_tables(te, rk, cnt, n_blocks, E):
    counts = cnt[0, :E]
    padded = (counts + MOE_BLOCK - 1) // MOE_BLOCK * MOE_BLOCK
    pad_end = jnp.cumsum(padded)
    pad_start = pad_end - padded
    experts = jnp.arange(E, dtype=jnp.int32)

    def lookup(table, idx):
        return jnp.sum(jnp.where(idx[..., None] == experts, table, 0), axis=-1)

    dest = (lookup(pad_start, te) + rk).reshape(-1).astype(jnp.int32)
    blk0 = jnp.arange(n_blocks, dtype=jnp.int32) * MOE_BLOCK
    block_e = jnp.minimum(jnp.sum(pad_end[None, :] <= blk0[:, None], axis=1), E - 1).astype(jnp.int32)
    n_valid = jnp.clip(lookup(counts, block_e) - (blk0 - lookup(pad_start, block_e)), 0, MOE_BLOCK).astype(jnp.int32)
    n_used = (pad_end[-1:] // MOE_BLOCK).astype(jnp.int32)
    return dest, block_e, n_valid, n_used


def _layer(x, b0, mod, tabs, band, p):
    B, S, D = x.shape
    N = B * S
    E = p['w_g'].shape[1]
    h = _norm_mod(x, p['norm_mix_g'], mod, b0, 1, 0)
    q, k = _proj_qk(h, p['w_qk'], p['q_norm_g'], p['k_norm_g'], tabs)
    vt, u, qn, kn, vn = _proj_mid(h, p['w_mid'], p['w_vt'])
    gates = _proj_gate(h.reshape(N, D), p['w_gate']).reshape(B, S, 3 * D)
    o_attn = _attention(q, k, vt)
    o_pool = _pool(u, band, p['w_pool'], p['pool_scale'])
    o_na = _na(qn, kn, vn, p['na_bias'])
    x = _merge(o_attn, o_pool, o_na, gates, x, mod, b0, p['w_br_attn'], p['w_br_pool'], p['w_br_na'], p['w_out'])
    h2, te, tw, rk, cnt = _ffn_prep(x, p['norm_ffn_g'], mod, b0, p['wr_hi'], p['wr_lo'], p['br'])
    n_blocks = -(-N * TOP_K // MOE_BLOCK) + E
    dest, block_e, n_valid, n_used = _slot_tables(te[..., :TOP_K], rk[..., :TOP_K], cnt, n_blocks, E)
    xs = _dispatch(h2.reshape(N, D), dest, n_valid)
    act = _gm1(xs, block_e, n_valid, n_used, p['layer'], p['w_g'], p['w_u'], p['b_g'], p['b_u'])
    y_slots = _gm2(act, block_e, n_valid, n_used, p['layer'], p['w_d'], p['b_d'])
    return _combine(x, tw, mod, b0, y_slots, dest)


def kernel(x_prompt, x_sample, c_prompt, c_sample, w_ada, b_ada, norm_mix_g, norm_ffn_g, w_in, q_norm_g, k_norm_g, w_pool, pool_scale, rpb, w_br_attn, w_br_pool, w_br_na, w_out, w_router, b_router, w_gu, b_gu, w_down, b_down, final_g):
    L, D, _ = w_ada.shape
    E = w_router.shape[-1]
    Fd = w_down.shape[2]
    Bp, Bs = c_prompt.shape[0], c_sample.shape[0]
    n_c = Bp + Bs
    c_all = jnp.zeros((-(-n_c // 8) * 8, D), F32).at[:n_c].set(jnp.concatenate([c_prompt, c_sample], axis=0))
    mod = _ada(c_all, w_ada, b_ada).reshape(L, c_all.shape[0], 6, 1, D)

    band = jnp.asarray(_pool_band(), BF16)
    w_g, w_u = _split_gu(w_gu)
    w_d = w_down.astype(BF16)
    b_gu2 = b_gu.reshape(L, E, Fd, 2)
    qk_w = ATTN_W + KV_W
    mid_w = sum(MID_SECTIONS)
    wr = jnp.zeros((L, D, LANES), F32).at[:, :, :E].set(w_router)
    wr_hi = wr.astype(BF16)
    wr_lo = (wr - wr_hi.astype(F32)).astype(BF16)
    br = jnp.full((L, 1, LANES), NEG, F32).at[:, 0, :E].set(b_router)

    layers = []
    for l in range(L):
        layers.append(dict(
            norm_mix_g=norm_mix_g[l], norm_ffn_g=norm_ffn_g[l],
            q_norm_g=q_norm_g[l], k_norm_g=k_norm_g[l],
            w_qk=w_in[l, :, :qk_w].astype(BF16),
            w_vt=w_in[l, :, qk_w:qk_w + KV_W].T.astype(BF16),
            w_mid=w_in[l, :, qk_w + KV_W:qk_w + KV_W + mid_w].astype(BF16),
            w_gate=w_in[l, :, qk_w + KV_W + mid_w:].astype(BF16),
            w_pool=w_pool[l].astype(BF16), pool_scale=pool_scale[l],
            na_bias=_na_bias(rpb[l]),
            w_br_attn=w_br_attn[l].astype(BF16), w_br_pool=w_br_pool[l].astype(BF16),
            w_br_na=w_br_na[l].astype(BF16), w_out=w_out[l].astype(BF16),
            wr_hi=wr_hi[l], wr_lo=wr_lo[l], br=br[l],
            layer=l, w_g=w_g, w_u=w_u,
            b_g=b_gu2[l, :, :, 0].reshape(E, 1, Fd), b_u=b_gu2[l, :, :, 1].reshape(E, 1, Fd),
            w_d=w_d, b_d=b_down[l].reshape(E, 1, D),
        ))

    outs = []
    for x, b0 in ((x_prompt, 0), (x_sample, Bp)):
        tabs = _rope_tables(x.shape[1])
        for l in range(L):
            x = _layer(x, b0, mod[l], tabs, band, layers[l])
        outs.append(_final_norm(x, final_g))
    return tuple(outs)
```

```python
import functools
import math

import numpy as np
import jax
import jax.numpy as jnp
from jax import lax
from jax.experimental import pallas as pl
from jax.experimental.pallas import tpu as pltpu

HEAD_DIM = 128
N_Q_HEADS = 8
N_KV_HEADS = 2
GQA_GROUP = N_Q_HEADS // N_KV_HEADS
ATTN_W = N_Q_HEADS * HEAD_DIM
KV_W = N_KV_HEADS * HEAD_DIM
GRID_W = 64
ROPE_THETA = 10000.0
ROPE_FREQS = HEAD_DIM // 4
POOL_WINDOWS = (2, 4, 8, 16)
N_POOL = len(POOL_WINDOWS)
POOL_DG = 128
POOL_W = N_POOL * POOL_DG
NA_HEADS = 4
NA_W = NA_HEADS * HEAD_DIM
NA_ROWS = 8
NA_COLS = 16
TOP_K = 4
MOE_BLOCK = 256
SWIGLU_ALPHA = 1.702
SWIGLU_LIMIT = 7.0
EPS = 1e-6

LANES = 128
POOL_HALO = 16
NEG = -1e30
VMEM_LIMIT = 56 * 1024 * 1024

F32 = jnp.float32
BF16 = jnp.bfloat16
_dot = functools.partial(jnp.dot, preferred_element_type=jnp.float32)


def _dot_nt(a, b):
    return lax.dot_general(a, b, (((1,), (1,)), ((), ())), preferred_element_type=jnp.float32)


def _cp(*sem, **kw):
    return pltpu.CompilerParams(dimension_semantics=sem, vmem_limit_bytes=VMEM_LIMIT, **kw)


def _split_bf16(a):
    hi = a.astype(BF16)
    lo = (a - hi.astype(F32)).astype(BF16)
    return hi, lo


def _resident(shape, index_map):
    return pl.BlockSpec(shape, index_map, pipeline_mode=pl.Buffered(1))


def _ada_kernel(c_ref, w_ref, b_ref, o_ref):
    c = c_ref[...]
    a = c * jax.nn.sigmoid(c)
    ah, al = _split_bf16(a)
    wh, wl = _split_bf16(w_ref[...])
    o_ref[...] = _dot(ah, wh) + _dot(al, wh) + _dot(ah, wl) + b_ref[...]


def _ada(c_all, w_ada, b_ada):
    L, D, D6 = w_ada.shape
    Bp = c_all.shape[0]
    tn = math.gcd(1024, D6)
    return pl.pallas_call(
        _ada_kernel,
        grid=(L, D6 // tn),
        in_specs=[pl.BlockSpec((Bp, D), lambda l, j: (0, 0)),
                  pl.BlockSpec((None, D, tn), lambda l, j: (l, 0, j)),
                  pl.BlockSpec((None, 1, tn), lambda l, j: (l, 0, j))],
        out_specs=pl.BlockSpec((None, Bp, tn), lambda l, j: (l, 0, j)),
        out_shape=jax.ShapeDtypeStruct((L, Bp, D6), F32),
        compiler_params=_cp("parallel", "parallel"),
        name="ada",
    )(c_all, w_ada, b_ada.reshape(L, 1, D6))


def _rms_mod(x, g, sc, sh):
    r = lax.rsqrt(jnp.mean(x * x, axis=-1, keepdims=True) + EPS)
    return (x * r * g) * (1.0 + sc) + sh


def _norm_mod_kernel(x_ref, g_ref, sc_ref, sh_ref, o_ref):
    o_ref[...] = _rms_mod(x_ref[...], g_ref[...], sc_ref[...], sh_ref[...]).astype(o_ref.dtype)


def _mod_spec(D, b0, which):
    return pl.BlockSpec((None, None, 1, D), lambda b, s: (b + b0, which, 0, 0))


def _norm_mod(x, g, mod, b0, i_sc, i_sh):
    B, S, D = x.shape
    ts = min(512, S)
    return pl.pallas_call(
        _norm_mod_kernel,
        grid=(B, S // ts),
        in_specs=[pl.BlockSpec((None, ts, D), lambda b, s: (b, s, 0)),
                  pl.BlockSpec((1, D), lambda b, s: (0, 0)),
                  _mod_spec(D, b0, i_sc), _mod_spec(D, b0, i_sh)],
        out_specs=pl.BlockSpec((None, ts, D), lambda b, s: (b, s, 0)),
        out_shape=jax.ShapeDtypeStruct((B, S, D), BF16),
        compiler_params=_cp("parallel", "parallel"),
        name="norm_mod",
    )(x, g.reshape(1, D), mod, mod)


def _final_norm_kernel(x_ref, g_ref, o_ref):
    x = x_ref[...]
    r = lax.rsqrt(jnp.mean(x * x, axis=-1, keepdims=True) + EPS)
    o_ref[...] = x * r * g_ref[...]


def _final_norm(x, g):
    B, S, D = x.shape
    ts = min(512, S)
    return pl.pallas_call(
        _final_norm_kernel,
        grid=(B, S // ts),
        in_specs=[pl.BlockSpec((None, ts, D), lambda b, s: (b, s, 0)),
                  pl.BlockSpec((1, D), lambda b, s: (0, 0))],
        out_specs=pl.BlockSpec((None, ts, D), lambda b, s: (b, s, 0)),
        out_shape=jax.ShapeDtypeStruct((B, S, D), F32),
        compiler_params=_cp("parallel", "parallel"),
        name="final_norm",
    )(x, g.reshape(1, D))


def _rope_tables(S):
    t = jnp.arange(S)
    pos = jnp.stack([t // GRID_W, t % GRID_W], axis=-1).astype(F32)
    inv = ROPE_THETA ** (-jnp.arange(ROPE_FREQS, dtype=F32) / ROPE_FREQS)
    ang = pos[..., None] * inv
    cos, sin = jnp.cos(ang), jnp.sin(ang)
    zero = jnp.zeros_like(sin)
    c = jnp.stack([cos, cos], axis=2).reshape(S, HEAD_DIM)
    sa = jnp.stack([-sin, zero], axis=2).reshape(S, HEAD_DIM)
    sb = jnp.stack([zero, sin], axis=2).reshape(S, HEAD_DIM)
    return c, sa, sb


def _proj_qk_kernel(a_ref, w_ref, gq_ref, gk_ref, c_ref, sa_ref, sb_ref, q_ref, k_ref):
    acc = _dot(a_ref[...], w_ref[...])
    c, sa, sb = c_ref[...], sa_ref[...], sb_ref[...]
    scale = math.log2(math.e) / math.sqrt(HEAD_DIM)
    heads = range(N_Q_HEADS + N_KV_HEADS)
    xs = [acc[:, hh * HEAD_DIM:(hh + 1) * HEAD_DIM] for hh in heads]
    rs = [lax.rsqrt(jnp.mean(x * x, axis=-1, keepdims=True) + EPS) for x in xs]
    xn = [xs[hh] * rs[hh] * (gq_ref[...] if hh < N_Q_HEADS else gk_ref[...]) for hh in heads]
    up = [pltpu.roll(x, HEAD_DIM - ROPE_FREQS, 1) for x in xn]
    dn = [pltpu.roll(x, ROPE_FREQS, 1) for x in xn]
    for hh in heads:
        y = xn[hh] * c + up[hh] * sa + dn[hh] * sb
        if hh < N_Q_HEADS:
            q_ref[:, hh * HEAD_DIM:(hh + 1) * HEAD_DIM] = (y * scale).astype(q_ref.dtype)
        else:
            kk = hh - N_Q_HEADS
            k_ref[:, kk * HEAD_DIM:(kk + 1) * HEAD_DIM] = y.astype(k_ref.dtype)


def _proj_qk(h, w_qk, gq, gk, tabs):
    B, S, D = h.shape
    tm = min(1024, S)
    W = ATTN_W + KV_W
    tab_spec = pl.BlockSpec((tm, HEAD_DIM), lambda b, s: (s, 0))
    vec_spec = pl.BlockSpec((1, HEAD_DIM), lambda b, s: (0, 0))
    return pl.pallas_call(
        _proj_qk_kernel,
        grid=(B, S // tm),
        in_specs=[pl.BlockSpec((None, tm, D), lambda b, s: (b, s, 0)),
                  _resident((D, W), lambda b, s: (0, 0)),
                  vec_spec, vec_spec, tab_spec, tab_spec, tab_spec],
        out_specs=[pl.BlockSpec((None, tm, ATTN_W), lambda b, s: (b, s, 0)),
                   pl.BlockSpec((None, tm, KV_W), lambda b, s: (b, s, 0))],
        out_shape=[jax.ShapeDtypeStruct((B, S, ATTN_W), BF16),
                   jax.ShapeDtypeStruct((B, S, KV_W), BF16)],
        compiler_params=_cp("parallel", "parallel"),
        name="proj_qk",
    )(h, w_qk, gq.reshape(1, HEAD_DIM), gk.reshape(1, HEAD_DIM), *tabs)


MID_SECTIONS = (POOL_W, NA_W, NA_W, NA_W)
ATTN_TK = 512


def _proj_mid_kernel(a_ref, w_ref, wvt_ref, vt_ref, *o_refs):
    a = a_ref[...]
    acc = _dot(a, w_ref[...])
    off = 0
    for o_ref, width in zip(o_refs, MID_SECTIONS):
        o_ref[...] = acc[:, off:off + width].astype(o_ref.dtype)
        off += width
    tk = vt_ref.shape[-1]
    for c in range(vt_ref.shape[0]):
        vt_ref[c] = _dot_nt(wvt_ref[...], a[c * tk:(c + 1) * tk]).astype(vt_ref.dtype)


def _proj_mid(h, w_mid, w_vt):
    B, S, D = h.shape
    tm = min(1024, S)
    tk = min(ATTN_TK, S)
    W = sum(MID_SECTIONS)
    return pl.pallas_call(
        _proj_mid_kernel,
        grid=(B, S // tm),
        in_specs=[pl.BlockSpec((None, tm, D), lambda b, s: (b, s, 0)),
                  _resident((D, W), lambda b, s: (0, 0)),
                  _resident((KV_W, D), lambda b, s: (0, 0))],
        out_specs=[pl.BlockSpec((None, tm // tk, KV_W, tk), lambda b, s: (b, s, 0, 0))]
        + [pl.BlockSpec((None, tm, w), lambda b, s: (b, s, 0)) for w in MID_SECTIONS],
        out_shape=[jax.ShapeDtypeStruct((B, S // tk, KV_W, tk), BF16)]
        + [jax.ShapeDtypeStruct((B, S, w), BF16) for w in MID_SECTIONS],
        compiler_params=_cp("parallel", "parallel"),
        name="proj_mid",
    )(h, w_mid, w_vt)


def _proj_gate_kernel(a_ref, w_ref, o_ref):
    o_ref[...] = jax.nn.sigmoid(_dot(a_ref[...], w_ref[...])).astype(o_ref.dtype)


def _proj_gate(h, w_gate):
    N, D = h.shape
    W = w_gate.shape[1]
    tm = min(1024, N)
    tn = min(2048, D)
    return pl.pallas_call(
        _proj_gate_kernel,
        grid=(W // tn, N // tm),
        in_specs=[pl.BlockSpec((tm, D), lambda j, i: (i, 0)),
                  pl.BlockSpec((D, tn), lambda j, i: (0, j))],
        out_specs=pl.BlockSpec((tm, tn), lambda j, i: (i, j)),
        out_shape=jax.ShapeDtypeStruct((N, W), BF16),
        compiler_params=_cp("parallel", "parallel"),
        name="proj_gate",
    )(h, w_gate)


def _attn_kernel(q_ref, k_ref, vt_ref, o_ref, s_ref):
    n_chunks, _, tk = vt_ref.shape
    tq = q_ref.shape[0]
    heads = range(GQA_GROUP)

    def scores(j, slot):
        k = k_ref[pl.ds(pl.multiple_of(j * tk, tk), tk), :]
        for h in heads:
            s_ref[slot, h] = _dot_nt(k, q_ref[:, h * HEAD_DIM:(h + 1) * HEAD_DIM])

    def consume(j, slot, carry):
        vt = vt_ref[j]
        m_new = [jnp.maximum(carry[h][0], jnp.max(s_ref[slot, h], axis=0, keepdims=True)) for h in heads]
        alpha = [jnp.exp2(carry[h][0] - m_new[h]) for h in heads]
        p = [jnp.exp2(s_ref[slot, h] - m_new[h]) for h in heads]
        l = [alpha[h] * carry[h][1] + jnp.sum(p[h], axis=0, keepdims=True) for h in heads]
        pv = [_dot(vt, p[h].astype(BF16)) for h in heads]
        return tuple((m_new[h], l[h], alpha[h] * carry[h][2] + pv[h]) for h in heads)

    init = tuple((jnp.full((1, tq), NEG, F32), jnp.zeros((1, tq), F32), jnp.zeros((HEAD_DIM, tq), F32))
                 for _ in heads)
    scores(0, 0)
    if n_chunks % 2:
        def body(j, carry):
            carry = consume(j, 0, carry)
            scores(jnp.minimum(j + 1, n_chunks - 1), 0)
            return carry
        final = lax.fori_loop(0, n_chunks, body, init)
    else:
        def body(jj, carry):
            j = 2 * jj
            scores(j + 1, 1)
            carry = consume(j, 0, carry)
            scores(jnp.minimum(j + 2, n_chunks - 1), 0)
            return consume(j + 1, 1, carry)
        final = lax.fori_loop(0, n_chunks // 2, body, init)
    for h in heads:
        m, l, acc = final[h]
        o_ref[:, h * HEAD_DIM:(h + 1) * HEAD_DIM] = (acc / l).T.astype(o_ref.dtype)


def _attention(q, k, vt):
    B, S, _ = q.shape
    n_chunks, _, tk = vt.shape[1:]
    tq = min(512, S)
    gw = GQA_GROUP * HEAD_DIM
    return pl.pallas_call(
        _attn_kernel,
        grid=(B, N_KV_HEADS, S // tq),
        in_specs=[pl.BlockSpec((None, tq, gw), lambda b, g, i: (b, i, g)),
                  pl.BlockSpec((None, S, HEAD_DIM), lambda b, g, i: (b, 0, g)),
                  pl.BlockSpec((None, n_chunks, HEAD_DIM, tk), lambda b, g, i: (b, 0, g, 0))],
        out_specs=pl.BlockSpec((None, tq, gw), lambda b, g, i: (b, i, g)),
        out_shape=jax.ShapeDtypeStruct((B, S, ATTN_W), BF16),
        scratch_shapes=[pltpu.VMEM((2, GQA_GROUP, tk, tq), F32)],
        compiler_params=_cp("parallel", "parallel", "parallel"),
        name="gqa",
    )(q, k, vt)


def _pool_band():
    T = MOE_BLOCK
    i = np.arange(T)[:, None] + POOL_HALO
    j = np.arange(T + 2 * POOL_HALO)[None, :]
    return np.stack([((j >= i - w // 2) & (j < i + w // 2)) for w in POOL_WINDOWS]).astype(np.float32)


def _pool_kernel(u_ref, band_ref, wp_ref, ps_ref, o_ref, *, T):
    S = u_ref.shape[0]
    nc = S // T
    halo = jnp.zeros((POOL_HALO, POOL_DG), BF16)
    for g in range(N_POOL):
        hw = POOL_WINDOWS[g] // 2
        cols = slice(g * POOL_DG, (g + 1) * POOL_DG)
        for c in range(nc):
            t0 = c * T
            mid = u_ref[t0:t0 + T, cols]
            prev = u_ref[t0 - POOL_HALO:t0, cols] if c > 0 else halo
            nxt = u_ref[t0 + T:t0 + T + POOL_HALO, cols] if c < nc - 1 else halo
            win = _dot(band_ref[g], jnp.concatenate([prev, mid, nxt], axis=0))
            t = t0 + lax.broadcasted_iota(jnp.int32, (T, POOL_DG), 0)
            cnt = (jnp.minimum(t + hw, S) - jnp.maximum(t - hw, 0)).astype(F32)
            pooled = win / cnt - mid.astype(F32)
            mixed = _dot(pooled.astype(BF16), wp_ref[g]) * ps_ref[:, cols]
            o_ref[t0:t0 + T, cols] = mixed.astype(o_ref.dtype)


def _pool(u, band, w_pool, pool_scale):
    B, S, _ = u.shape
    T = band.shape[1]
    return pl.pallas_call(
        functools.partial(_pool_kernel, T=T),
        grid=(B,),
        in_specs=[pl.BlockSpec((None, S, POOL_W), lambda b: (b, 0, 0)),
                  pl.BlockSpec(band.shape, lambda b: (0, 0, 0)),
                  pl.BlockSpec(w_pool.shape, lambda b: (0, 0, 0)),
                  pl.BlockSpec((1, POOL_W), lambda b: (0, 0))],
        out_specs=pl.BlockSpec((None, S, POOL_W), lambda b: (b, 0, 0)),
        out_shape=jax.ShapeDtypeStruct((B, S, POOL_W), BF16),
        compiler_params=_cp("parallel"),
        name="pool",
    )(u, band, w_pool, pool_scale.reshape(1, POOL_W))


def _na_bias(rpb):
    H = rpb.shape[0]
    nr, nc, P = 2 * NA_ROWS - 1, 2 * NA_COLS - 1, 2 * GRID_W
    w = jnp.full((H, nr, P), NEG, F32).at[:, :, GRID_W - NA_COLS:GRID_W - 1 + NA_COLS].set(rpb.reshape(H, nr, nc))
    skew = jnp.tile(w, (1, 1, GRID_W))[:, :, :GRID_W * (P - 1)].reshape(H, nr, GRID_W, P - 1)
    toe = skew[..., GRID_W - 1:2 * GRID_W - 1]
    qc = np.arange(GRID_W)[:, None]
    kc = np.arange(GRID_W)[None, :]
    cs = np.clip(qc - NA_COLS // 2, 0, GRID_W - NA_COLS)
    toe = jnp.where((kc >= cs) & (kc < cs + NA_COLS), toe, NEG)
    per_off = jnp.stack([toe[:, NA_ROWS - 1 - o:2 * NA_ROWS - 1 - o] for o in range(NA_ROWS)], axis=1)
    return per_off.transpose(0, 1, 3, 2, 4).reshape(H, NA_ROWS, GRID_W, NA_ROWS * GRID_W)


def _na_kernel(q_ref, k_ref, v_ref, b_ref, o_ref, *, rb, R):
    i = pl.program_id(1)
    scale = 1.0 / math.sqrt(HEAD_DIM)
    win = NA_ROWS * GRID_W
    def window(rr):
        r = i * rb + rr
        rs = jnp.clip(r - NA_ROWS // 2, 0, R - NA_ROWS)
        return r - rs, pl.multiple_of(rs * GRID_W, GRID_W)

    def scores(rr, h):
        off, start = window(rr)
        cols = slice(h * HEAD_DIM, (h + 1) * HEAD_DIM)
        q = q_ref[rr * GRID_W:(rr + 1) * GRID_W, cols]
        return _dot_nt(q, k_ref[pl.ds(start, win), cols]) * scale + b_ref[h, off]

    def finish(rr, h, s):
        _, start = window(rr)
        cols = slice(h * HEAD_DIM, (h + 1) * HEAD_DIM)
        m = jnp.max(s, axis=-1, keepdims=True)
        p = jnp.exp(s - m)
        l = jnp.sum(p, axis=-1, keepdims=True)
        out = _dot(p.astype(BF16), v_ref[pl.ds(start, win), cols]) / l
        o_ref[rr * GRID_W:(rr + 1) * GRID_W, cols] = out.astype(o_ref.dtype)

    todo = [(rr, h) for rr in range(rb) for h in range(NA_HEADS)]
    s_next = scores(*todo[0])
    for n, (rr, h) in enumerate(todo):
        s_cur = s_next
        if n + 1 < len(todo):
            s_next = scores(*todo[n + 1])
        finish(rr, h, s_cur)


def _na(q, k, v, bias):
    B, S, _ = q.shape
    R = S // GRID_W
    assert R >= NA_ROWS
    rb = 8
    return pl.pallas_call(
        functools.partial(_na_kernel, rb=rb, R=R),
        grid=(B, R // rb),
        in_specs=[pl.BlockSpec((None, rb * GRID_W, NA_W), lambda b, i: (b, i, 0)),
                  pl.BlockSpec((None, S, NA_W), lambda b, i: (b, 0, 0)),
                  pl.BlockSpec((None, S, NA_W), lambda b, i: (b, 0, 0)),
                  pl.BlockSpec(bias.shape, lambda b, i: (0, 0, 0, 0))],
        out_specs=pl.BlockSpec((None, rb * GRID_W, NA_W), lambda b, i: (b, i, 0)),
        out_shape=jax.ShapeDtypeStruct((B, S, NA_W), BF16),
        compiler_params=_cp("parallel", "parallel"),
        name="natten",
    )(q, k, v, bias)


def _merge_kernel(oa_ref, op_ref, on_ref, ga_ref, gp_ref, gn_ref, x_ref, g1_ref,
                  wa_ref, wp_ref, wn_ref, wo_ref, o_ref):
    m = ga_ref[...].astype(F32) * _dot(oa_ref[...], wa_ref[...])
    m = m + gp_ref[...].astype(F32) * _dot(op_ref[...], wp_ref[...])
    m = m + gn_ref[...].astype(F32) * _dot(on_ref[...], wn_ref[...])
    y = _dot(m.astype(BF16), wo_ref[...])
    o_ref[...] = x_ref[...] + g1_ref[...] * y


def _merge(o_attn, o_pool, o_na, gates, x, mod, b0, w_a, w_p, w_n, w_o):
    B, S, D = x.shape
    tm = min(512, S)
    row = lambda w: pl.BlockSpec((None, tm, w), lambda b, s: (b, s, 0))
    gate = lambda j: pl.BlockSpec((None, tm, D), lambda b, s: (b, s, j))
    res = lambda w: _resident(w.shape, lambda b, s: (0, 0))
    return pl.pallas_call(
        _merge_kernel,
        grid=(B, S // tm),
        in_specs=[row(ATTN_W), row(POOL_W), row(NA_W), gate(0), gate(1), gate(2), row(D),
                  _mod_spec(D, b0, 2), res(w_a), res(w_p), res(w_n), res(w_o)],
        out_specs=row(D),
        out_shape=jax.ShapeDtypeStruct((B, S, D), F32),
        compiler_params=_cp("parallel", "parallel"),
        name="merge",
    )(o_attn, o_pool, o_na, gates, gates, gates, x, mod, w_a, w_p, w_n, w_o)


def _ffn_prep_kernel(x_ref, g_ref, sc_ref, sh_ref, wrh_ref, wrl_ref, br_ref,
                     h_ref, te_ref, tw_ref, rk_ref, cnt_ref, carry_ref):
    first = jnp.logical_and(pl.program_id(0) == 0, pl.program_id(1) == 0)

    @pl.when(first)
    def _():
        carry_ref[...] = jnp.zeros_like(carry_ref)

    tm = x_ref.shape[0]
    h = _rms_mod(x_ref[...], g_ref[...], sc_ref[...], sh_ref[...])
    h_ref[...] = h
    hh, hl = _split_bf16(h)
    wrh = wrh_ref[...]
    logits = _dot(hh, wrh) + _dot(hl, wrh) + _dot(hh, wrl_ref[...]) + br_ref[...]
    lane = lax.broadcasted_iota(jnp.int32, (tm, LANES), 1).astype(F32)
    vals, idxs, hots = [], [], []
    cur = logits
    for _ in range(TOP_K):
        m = jnp.max(cur, axis=-1, keepdims=True)
        idx = jnp.min(jnp.where(cur == m, lane, float(LANES)), axis=-1, keepdims=True)
        hot = lane == idx
        vals.append(m)
        idxs.append(idx)
        hots.append(hot)
        cur = jnp.where(hot, -3e38, cur)
    ex = [jnp.exp(v - vals[0]) for v in vals]
    den = ex[0] + ex[1] + ex[2] + ex[3]
    onehot = sum(hh_.astype(F32) for hh_ in hots)
    ri = lax.broadcasted_iota(jnp.int32, (tm, tm), 0)
    ci = lax.broadcasted_iota(jnp.int32, (tm, tm), 1)
    tri = (ri > ci).astype(BF16)
    carry = carry_ref[0:1, :]
    before = _dot(tri, onehot.astype(BF16)) + carry
    te = jnp.zeros((tm, LANES), F32)
    tw = jnp.zeros((tm, LANES), F32)
    rk = jnp.zeros((tm, LANES), F32)
    for k in range(TOP_K):
        sel = lane == float(k)
        rank = jnp.sum(jnp.where(hots[k], before, 0.0), axis=-1, keepdims=True)
        te = jnp.where(sel, idxs[k], te)
        tw = jnp.where(sel, ex[k] / den, tw)
        rk = jnp.where(sel, rank, rk)
    te_ref[...] = te.astype(jnp.int32)
    tw_ref[...] = tw
    rk_ref[...] = rk.astype(jnp.int32)
    total = carry + jnp.sum(onehot, axis=0, keepdims=True)
    carry_ref[...] = jnp.broadcast_to(total, carry_ref.shape)
    cnt_ref[...] = jnp.broadcast_to(total, cnt_ref.shape).astype(jnp.int32)


def _ffn_prep(x, g, mod, b0, wr_hi, wr_lo, br):
    B, S, D = x.shape
    tm = min(512, S)
    row = lambda w: pl.BlockSpec((None, tm, w), lambda b, s: (b, s, 0))
    const = lambda shape: pl.BlockSpec(shape, lambda b, s: (0, 0))
    return pl.pallas_call(
        _ffn_prep_kernel,
        grid=(B, S // tm),
        in_specs=[row(D), const((1, D)), _mod_spec(D, b0, 4), _mod_spec(D, b0, 3),
                  const((D, LANES)), const((D, LANES)), const((1, LANES))],
        out_specs=[row(D), row(LANES), row(LANES), row(LANES), const((8, LANES))],
        out_shape=[jax.ShapeDtypeStruct((B, S, D), F32),
                   jax.ShapeDtypeStruct((B, S, LANES), jnp.int32),
                   jax.ShapeDtypeStruct((B, S, LANES), F32),
                   jax.ShapeDtypeStruct((B, S, LANES), jnp.int32),
                   jax.ShapeDtypeStruct((8, LANES), jnp.int32)],
        scratch_shapes=[pltpu.VMEM((8, LANES), F32)],
        compiler_params=_cp("arbitrary", "arbitrary"),
        name="ffn_prep",
    )(x, g.reshape(1, D), mod, mod, wr_hi, wr_lo, br)


def _dispatch_kernel(nv_ref, dest_ref, h_ref, xs_ref, zeros, sem, zsem):
    tm = h_ref.shape[0]
    nb = nv_ref.shape[0]

    @pl.when(pl.program_id(0) == 0)
    def _():
        zeros[...] = jnp.zeros_like(zeros)

        def tail_copies(b, fn):
            nv = nv_ref[b]
            base = b * MOE_BLOCK
            up = (nv + 7) // 8 * 8
            for r in range(7):
                @pl.when(nv + r < up)
                def _(r=r):
                    fn(pltpu.make_async_copy(zeros.at[pl.ds(0, 1), :], xs_ref.at[pl.ds(base + nv + r, 1), :], zsem))
            units = (MOE_BLOCK - up) // 8
            off = up
            for bit in (32, 16, 8, 4, 2, 1):
                size = bit * 8
                hit = (units & bit) != 0

                @pl.when(hit)
                def _(size=size, off=off):
                    dst = xs_ref.at[pl.ds(pl.multiple_of(base + off, 8), size), :]
                    fn(pltpu.make_async_copy(zeros.at[pl.ds(0, size), :], dst, zsem))
                off = off + jnp.where(hit, size, 0)

        def start_all(b, c):
            tail_copies(b, lambda cp: cp.start())
            return c

        def wait_all(b, c):
            tail_copies(b, lambda cp: cp.wait())
            return c

        lax.fori_loop(0, nb, start_all, 0)
        lax.fori_loop(0, nb, wait_all, 0)

    def row_copy(r, d):
        return pltpu.make_async_copy(h_ref.at[pl.ds(r, 1), :], xs_ref.at[pl.ds(d, 1), :], sem)

    def issue(r, c):
        for k in range(TOP_K):
            row_copy(r, dest_ref[r * TOP_K + k]).start(priority=k % 2)
        return c

    lax.fori_loop(0, tm, issue, 0)

    def drain(r, c):
        row_copy(0, 0).wait()
        return c

    lax.fori_loop(0, tm * TOP_K, drain, 0, unroll=8)


def _dispatch(h, dest, n_valid):
    N, D = h.shape
    tm = min(512, N)
    n_slots = n_valid.shape[0] * MOE_BLOCK
    return pl.pallas_call(
        _dispatch_kernel,
        grid_spec=pltpu.PrefetchScalarGridSpec(
            num_scalar_prefetch=1, grid=(N // tm,),
            in_specs=[pl.BlockSpec((tm * TOP_K,), lambda i, nv: (i,), memory_space=pltpu.SMEM),
                      pl.BlockSpec((tm, D), lambda i, nv: (i, 0))],
            out_specs=pl.BlockSpec(memory_space=pl.ANY),
            scratch_shapes=[pltpu.VMEM((MOE_BLOCK, D), F32), pltpu.SemaphoreType.DMA(()),
                            pltpu.SemaphoreType.DMA(())]),
        out_shape=jax.ShapeDtypeStruct((n_slots, D), F32),
        compiler_params=_cp("arbitrary", has_side_effects=True),
        name="dispatch",
    )(n_valid, dest, h)


SPLIT_CHUNK = 2 * LANES


def _split_gu_kernel(w_ref, sel_ref, g_ref, u_ref):
    sel = sel_ref[...]
    for c in range(w_ref.shape[1] // SPLIT_CHUNK):
        chunk = w_ref[:, c * SPLIT_CHUNK:(c + 1) * SPLIT_CHUNK].astype(BF16)
        r = _dot(chunk, sel)
        g_ref[:, c * LANES:(c + 1) * LANES] = r[:, :LANES].astype(g_ref.dtype)
        u_ref[:, c * LANES:(c + 1) * LANES] = r[:, LANES:].astype(u_ref.dtype)


def _split_gu(w_gu):
    L, E, D, F2 = w_gu.shape
    Fd = F2 // 2
    tk = min(256, D)
    src = np.arange(SPLIT_CHUNK)[:, None]
    dst = np.arange(SPLIT_CHUNK)[None, :]
    sel = jnp.asarray(src == 2 * (dst % LANES) + dst // LANES, BF16)
    out_spec = pl.BlockSpec((None, None, tk, Fd), lambda l, e, k: (l, e, k, 0))
    return pl.pallas_call(
        _split_gu_kernel,
        grid=(L, E, D // tk),
        in_specs=[pl.BlockSpec((None, None, tk, F2), lambda l, e, k: (l, e, k, 0)),
                  pl.BlockSpec((SPLIT_CHUNK, SPLIT_CHUNK), lambda l, e, k: (0, 0))],
        out_specs=[out_spec, out_spec],
        out_shape=[jax.ShapeDtypeStruct((L, E, D, Fd), BF16)] * 2,
        compiler_params=_cp("parallel", "parallel", "parallel"),
        name="split_gate_up",
    )(w_gu, sel)


def _gm1_kernel(be_ref, nv_ref, nu_ref, x_ref, wg_ref, wu_ref, bg_ref, bu_ref, o_ref):
    i = pl.program_id(0)

    @pl.when(i < nu_ref[0])
    def _():
        rows = lax.broadcasted_iota(jnp.int32, x_ref.shape, 0)
        x = jnp.where(rows < nv_ref[i], x_ref[...], 0.0).astype(BF16)
        g = _dot(x, wg_ref[...]) + bg_ref[...]
        u = _dot(x, wu_ref[...]) + bu_ref[...]
        gate = jnp.minimum(g, SWIGLU_LIMIT)
        up = jnp.clip(u, -SWIGLU_LIMIT, SWIGLU_LIMIT)
        o_ref[...] = ((up + 1.0) * gate * jax.nn.sigmoid(SWIGLU_ALPHA * gate)).astype(o_ref.dtype)

    @pl.when(i >= nu_ref[0])
    def _():
        o_ref[...] = jnp.zeros_like(o_ref)


def _gm1(xs, block_e, n_valid, n_used, layer, w_g, w_u, b_g, b_u):
    n_slots, D = xs.shape
    Fd = w_g.shape[-1]
    nb = n_slots // MOE_BLOCK
    wspec = pl.BlockSpec((None, None, D, Fd), lambda i, be, nv, nu: (layer, be[i], 0, 0))
    bspec = pl.BlockSpec((None, 1, Fd), lambda i, be, nv, nu: (be[i], 0, 0))
    return pl.pallas_call(
        _gm1_kernel,
        grid_spec=pltpu.PrefetchScalarGridSpec(
            num_scalar_prefetch=3, grid=(nb,),
            in_specs=[pl.BlockSpec((MOE_BLOCK, D), lambda i, be, nv, nu: (i, 0)), wspec, wspec, bspec, bspec],
            out_specs=pl.BlockSpec((MOE_BLOCK, Fd), lambda i, be, nv, nu: (i, 0))),
        out_shape=jax.ShapeDtypeStruct((n_slots, Fd), BF16),
        compiler_params=_cp("arbitrary"),
        name="expert_up",
    )(block_e, n_valid, n_used, xs, w_g, w_u, b_g, b_u)


def _gm2_kernel(be_ref, nu_ref, a_ref, wd_ref, bd_ref, o_ref):
    i = pl.program_id(0)

    @pl.when(i < nu_ref[0])
    def _():
        o_ref[...] = _dot(a_ref[...], wd_ref[...]) + bd_ref[...]

    @pl.when(i >= nu_ref[0])
    def _():
        o_ref[...] = jnp.zeros_like(o_ref)


def _gm2(act, block_e, n_used, layer, w_d, b_d):
    n_slots, Fd = act.shape
    D = w_d.shape[-1]
    nb = n_slots // MOE_BLOCK
    return pl.pallas_call(
        _gm2_kernel,
        grid_spec=pltpu.PrefetchScalarGridSpec(
            num_scalar_prefetch=2, grid=(nb,),
            in_specs=[pl.BlockSpec((MOE_BLOCK, Fd), lambda i, be, nu: (i, 0)),
                      pl.BlockSpec((None, None, Fd, D), lambda i, be, nu: (layer, be[i], 0, 0)),
                      pl.BlockSpec((None, 1, D), lambda i, be, nu: (be[i], 0, 0))],
            out_specs=pl.BlockSpec((MOE_BLOCK, D), lambda i, be, nu: (i, 0))),
        out_shape=jax.ShapeDtypeStruct((n_slots, D), F32),
        compiler_params=_cp("arbitrary"),
        name="expert_down",
    )(block_e, n_used, act, w_d, b_d)


def _combine_kernel(dest_ref, x_ref, tw_ref, g2_ref, y_ref, o_ref, buf, sem):
    tm = x_ref.shape[0]

    def row_copy(r, k, d):
        return pltpu.make_async_copy(y_ref.at[pl.ds(d, 1), :], buf.at[k, pl.ds(r, 1), :], sem)

    def issue(r, c):
        for k in range(TOP_K):
            row_copy(r, k, dest_ref[r * TOP_K + k]).start(priority=k % 2)
        return c

    lax.fori_loop(0, tm, issue, 0)

    def drain(r, c):
        row_copy(0, 0, 0).wait()
        return c

    lax.fori_loop(0, tm * TOP_K, drain, 0, unroll=8)
    tw = tw_ref[...]
    acc = tw[:, 0:1] * buf[0]
    for k in range(1, TOP_K):
        acc = acc + tw[:, k:k + 1] * buf[k]
    o_ref[...] = x_ref[...] + g2_ref[...] * acc


def _combine(x, tw, mod, b0, y_slots, dest):
    B, S, D = x.shape
    tm = min(256, S)
    spb = S // tm
    row = lambda w: pl.BlockSpec((None, tm, w), lambda b, s: (b, s, 0))
    return pl.pallas_call(
        _combine_kernel,
        grid=(B, spb),
        in_specs=[pl.BlockSpec((tm * TOP_K,), lambda b, s: (b * spb + s,), memory_space=pltpu.SMEM),
                  row(D), row(LANES), _mod_spec(D, b0, 5),
                  pl.BlockSpec(memory_space=pl.ANY)],
        out_specs=row(D),
        out_shape=jax.ShapeDtypeStruct((B, S, D), F32),
        scratch_shapes=[pltpu.VMEM((TOP_K, tm, D), F32), pltpu.SemaphoreType.DMA(())],
        compiler_params=_cp("arbitrary", "arbitrary"),
        name="combine",
    )(dest, x, tw, mod, y_slots)


def _slot_tables(te, rk, cnt, n_blocks, E):
    counts = cnt[0, :E]
    padded = (counts + MOE_BLOCK - 1) // MOE_BLOCK * MOE_BLOCK
    pad_end = jnp.cumsum(padded)
    pad_start = pad_end - padded
    experts = jnp.arange(E, dtype=jnp.int32)

    def lookup(table, idx):
        return jnp.sum(jnp.where(idx[..., None] == experts, table, 0), axis=-1)

    dest = (lookup(pad_start, te) + rk).reshape(-1).astype(jnp.int32)
    blk0 = jnp.arange(n_blocks, dtype=jnp.int32) * MOE_BLOCK
    block_e = jnp.minimum(jnp.sum(pad_end[None, :] <= blk0[:, None], axis=1), E - 1).astype(jnp.int32)
    n_valid = jnp.clip(lookup(counts, block_e) - (blk0 - lookup(pad_start, block_e)), 0, MOE_BLOCK).astype(jnp.int32)
    n_used = (pad_end[-1:] // MOE_BLOCK).astype(jnp.int32)
    return dest, block_e, n_valid, n_used


def _layer(x, b0, mod, tabs, band, p):
    B, S, D = x.shape
    N = B * S
    E = p['w_g'].shape[1]
    h = _norm_mod(x, p['norm_mix_g'], mod, b0, 1, 0)
    q, k = _proj_qk(h, p['w_qk'], p['q_norm_g'], p['k_norm_g'], tabs)
    vt, u, qn, kn, vn = _proj_mid(h, p['w_mid'], p['w_vt'])
    gates = _proj_gate(h.reshape(N, D), p['w_gate']).reshape(B, S, 3 * D)
    o_attn = _attention(q, k, vt)
    o_pool = _pool(u, band, p['w_pool'], p['pool_scale'])
    o_na = _na(qn, kn, vn, p['na_bias'])
    x = _merge(o_attn, o_pool, o_na, gates, x, mod, b0, p['w_br_attn'], p['w_br_pool'], p['w_br_na'], p['w_out'])
    h2, te, tw, rk, cnt = _ffn_prep(x, p['norm_ffn_g'], mod, b0, p['wr_hi'], p['wr_lo'], p['br'])
    n_blocks = -(-N * TOP_K // MOE_BLOCK) + E
    dest, block_e, n_valid, n_used = _slot_tables(te[..., :TOP_K], rk[..., :TOP_K], cnt, n_blocks, E)
    xs = _dispatch(h2.reshape(N, D), dest, n_valid)
    act = _gm1(xs, block_e, n_valid, n_used, p['layer'], p['w_g'], p['w_u'], p['b_g'], p['b_u'])
    y_slots = _gm2(act, block_e, n_used, p['layer'], p['w_d'], p['b_d'])
    return _combine(x, tw, mod, b0, y_slots, dest)


def kernel(x_prompt, x_sample, c_prompt, c_sample, w_ada, b_ada, norm_mix_g, norm_ffn_g, w_in, q_norm_g, k_norm_g, w_pool, pool_scale, rpb, w_br_attn, w_br_pool, w_br_na, w_out, w_router, b_router, w_gu, b_gu, w_down, b_down, final_g):
    L, D, _ = w_ada.shape
    E = w_router.shape[-1]
    Fd = w_down.shape[2]
    Bp, Bs = c_prompt.shape[0], c_sample.shape[0]
    n_c = Bp + Bs
    c_all = jnp.zeros((-(-n_c // 8) * 8, D), F32).at[:n_c].set(jnp.concatenate([c_prompt, c_sample], axis=0))
    mod = _ada(c_all, w_ada, b_ada).reshape(L, c_all.shape[0], 6, 1, D)

    band = jnp.asarray(_pool_band(), BF16)
    w_g, w_u = _split_gu(w_gu)
    w_d = w_down.astype(BF16)
    b_gu2 = b_gu.reshape(L, E, Fd, 2)
    qk_w = ATTN_W + KV_W
    mid_w = sum(MID_SECTIONS)
    wr = jnp.zeros((L, D, LANES), F32).at[:, :, :E].set(w_router)
    wr_hi = wr.astype(BF16)
    wr_lo = (wr - wr_hi.astype(F32)).astype(BF16)
    br = jnp.full((L, 1, LANES), NEG, F32).at[:, 0, :E].set(b_router)

    layers = []
    for l in range(L):
        layers.append(dict(
            norm_mix_g=norm_mix_g[l], norm_ffn_g=norm_ffn_g[l],
            q_norm_g=q_norm_g[l], k_norm_g=k_norm_g[l],
            w_qk=w_in[l, :, :qk_w].astype(BF16),
            w_vt=w_in[l, :, qk_w:qk_w + KV_W].T.astype(BF16),
            w_mid=w_in[l, :, qk_w + KV_W:qk_w + KV_W + mid_w].astype(BF16),
            w_gate=w_in[l, :, qk_w + KV_W + mid_w:].astype(BF16),
            w_pool=w_pool[l].astype(BF16), pool_scale=pool_scale[l],
            na_bias=_na_bias(rpb[l]),
            w_br_attn=w_br_attn[l].astype(BF16), w_br_pool=w_br_pool[l].astype(BF16),
            w_br_na=w_br_na[l].astype(BF16), w_out=w_out[l].astype(BF16),
            wr_hi=wr_hi[l], wr_lo=wr_lo[l], br=br[l],
            layer=l, w_g=w_g, w_u=w_u,
            b_g=b_gu2[l, :, :, 0].reshape(E, 1, Fd), b_u=b_gu2[l, :, :, 1].reshape(E, 1, Fd),
            w_d=w_d, b_d=b_down[l].reshape(E, 1, D),
        ))

    outs = []
    for x, b0 in ((x_prompt, 0), (x_sample, Bp)):
        tabs = _rope_tables(x.shape[1])
        for l in range(L):
            x = _layer(x, b0, mod[l], tabs, band, layers[l])
        outs.append(_final_norm(x, final_g))
    return tuple(outs)
```
